```python
import jax, jax.numpy as jnp
from jax import lax
import numpy as np

D_MODEL = 1024
BATCH = 32
SEQ = 2048
DEPTH = 1

CONV_WIDTH = D_MODEL
CONV_K = 3
N_Q_HEADS = 16
N_KV_HEADS = 2
HEAD_DIM = 64
ATTN_WIDTH = N_Q_HEADS * HEAD_DIM
KV_WIDTH = N_KV_HEADS * HEAD_DIM
GROUP = N_Q_HEADS // N_KV_HEADS
WINDOW = 128
BLOCK = 128
PROJ_WIDTH = 3 * CONV_WIDTH + ATTN_WIDTH + 2 * KV_WIDTH
D_FF = 2816
FFN_RESIDUAL = 0.5
N_BRANCHES = 2
EPS = 1e-6
NEG_INF = -1e30

kernel_name = "hybrid_shortconv_swa_sink_macaron_block"


def rms_norm(x, w):
    xf = x.astype(jnp.float32)
    y = xf * lax.rsqrt(jnp.mean(xf * xf, axis=-1, keepdims=True) + EPS)
    return (y * w.astype(jnp.float32)).astype(x.dtype)


def swiglu(x, w_gate, w_up, w_down):
    return (jax.nn.silu(x @ w_gate) * (x @ w_up)) @ w_down


def causal_depthwise_conv(z, w):
    S = z.shape[1]
    zp = jnp.pad(z, ((0, 0), (CONV_K - 1, 0), (0, 0)))
    y = zp[:, 0:S] * w[0]
    for k in range(1, CONV_K):
        y = y + zp[:, k:k + S] * w[k]
    return y


def sliding_window_attention(q, k, v, sinks):
    B, S = q.shape[0], q.shape[1]
    nb = S // BLOCK
    qb = q.reshape(B, nb, BLOCK, N_KV_HEADS, GROUP, HEAD_DIM)
    pad = jnp.zeros((B, BLOCK, N_KV_HEADS, HEAD_DIM), k.dtype)
    kp = jnp.concatenate([pad, k], axis=1).reshape(B, nb + 1, BLOCK, N_KV_HEADS, HEAD_DIM)
    vp = jnp.concatenate([pad, v], axis=1).reshape(B, nb + 1, BLOCK, N_KV_HEADS, HEAD_DIM)
    kb = jnp.concatenate([kp[:, :-1], kp[:, 1:]], axis=2)
    vb = jnp.concatenate([vp[:, :-1], vp[:, 1:]], axis=2)
    scale = HEAD_DIM ** -0.5
    scores = jnp.einsum('bnqhgd,bnkhd->bnhgqk', qb, kb).astype(jnp.float32) * scale
    qpos = jnp.arange(BLOCK)[:, None] + BLOCK
    kpos = jnp.arange(2 * BLOCK)[None, :]
    band = (kpos <= qpos) & (qpos - kpos < WINDOW)
    not_pad = (jnp.arange(nb)[:, None, None] > 0) | (kpos[None] >= BLOCK)
    mask = band[None] & not_pad
    scores = jnp.where(mask[None, :, None, None], scores, NEG_INF)
    sink = sinks.astype(jnp.float32).reshape(N_KV_HEADS, GROUP)[None, None, :, :, None, None]
    sink = jnp.broadcast_to(sink, scores.shape[:-1] + (1,))
    probs = jax.nn.softmax(jnp.concatenate([scores, sink], axis=-1), axis=-1)[..., :-1]
    out = jnp.einsum('bnhgqk,bnkhd->bnqhgd', probs.astype(v.dtype), vb)
    return out.reshape(B, S, ATTN_WIDTH)


def setup_inputs(seed: int = 0) -> dict:
    key = jax.random.key(seed)
    ks = jax.random.split(key, 24)
    L, D = DEPTH, D_MODEL

    def w(k, shape, fan_in):
        return jax.random.normal(k, shape, jnp.float32) * fan_in ** -0.5

    def gain(k, shape):
        return 1.0 + 0.05 * jax.random.normal(k, shape, jnp.float32)

    return {
        "x": jax.random.normal(ks[0], (BATCH, SEQ, D), jnp.float32),
        "ffn1_norm": gain(ks[1], (L, D)),
        "ffn1_w_gate": w(ks[2], (L, D, D_FF), D),
        "ffn1_w_up": w(ks[3], (L, D, D_FF), D),
        "ffn1_w_down": w(ks[4], (L, D_FF, D), D_FF),
        "mix_norm": gain(ks[5], (L, D)),
        "w_in": w(ks[6], (L, D, PROJ_WIDTH), D),
        "conv_w": w(ks[7], (L, CONV_K, CONV_WIDTH), CONV_K),
        "q_norm": gain(ks[8], (L, HEAD_DIM)),
        "k_norm": gain(ks[9], (L, HEAD_DIM)),
        "sinks": 0.5 * jax.random.normal(ks[10], (L, N_Q_HEADS), jnp.float32),
        "w_conv_out": w(ks[11], (L, CONV_WIDTH, D), CONV_WIDTH),
        "w_attn_out": w(ks[12], (L, ATTN_WIDTH, D), ATTN_WIDTH),
        "w_branch_gate": w(ks[13], (L, D, N_BRANCHES * D), D),
        "b_branch_gate": 0.02 * jax.random.normal(ks[14], (L, N_BRANCHES * D), jnp.float32),
        "w_out": w(ks[15], (L, D, D), D),
        "ffn2_norm": gain(ks[16], (L, D)),
        "ffn2_w_gate": w(ks[17], (L, D, D_FF), D),
        "ffn2_w_up": w(ks[18], (L, D, D_FF), D),
        "ffn2_w_down": w(ks[19], (L, D_FF, D), D_FF),
    }


def reference(x, ffn1_norm, ffn1_w_gate, ffn1_w_up, ffn1_w_down, mix_norm, w_in, conv_w,
              q_norm, k_norm, sinks, w_conv_out, w_attn_out, w_branch_gate, b_branch_gate,
              w_out, ffn2_norm, ffn2_w_gate, ffn2_w_up, ffn2_w_down):
    B, S = x.shape[0], x.shape[1]
    for l in range(DEPTH):
        x = x + FFN_RESIDUAL * swiglu(rms_norm(x, ffn1_norm[l]), ffn1_w_gate[l], ffn1_w_up[l], ffn1_w_down[l])

        h = rms_norm(x, mix_norm[l])
        p = h @ w_in[l]
        c0 = 3 * CONV_WIDTH
        gb, gc, vc = jnp.split(p[..., :c0], 3, axis=-1)
        q = p[..., c0:c0 + ATTN_WIDTH].reshape(B, S, N_Q_HEADS, HEAD_DIM)
        k = p[..., c0 + ATTN_WIDTH:c0 + ATTN_WIDTH + KV_WIDTH].reshape(B, S, N_KV_HEADS, HEAD_DIM)
        v = p[..., c0 + ATTN_WIDTH + KV_WIDTH:].reshape(B, S, N_KV_HEADS, HEAD_DIM)

        y_conv = (gb * causal_depthwise_conv(gc * vc, conv_w[l])) @ w_conv_out[l]

        q = rms_norm(q, q_norm[l])
        k = rms_norm(k, k_norm[l])
        y_attn = sliding_window_attention(q, k, v, sinks[l]) @ w_attn_out[l]

        g = jax.nn.sigmoid(h @ w_branch_gate[l] + b_branch_gate[l]).reshape(B, S, N_BRANCHES, D_MODEL)
        merged = g[:, :, 0] * y_conv + g[:, :, 1] * y_attn
        x = x + merged @ w_out[l]

        x = x + FFN_RESIDUAL * swiglu(rms_norm(x, ffn2_norm[l]), ffn2_w_gate[l], ffn2_w_up[l], ffn2_w_down[l])
    return x
```

```python
import functools

import jax
import jax.numpy as jnp
from jax import lax
from jax.experimental import pallas as pl
from jax.experimental.pallas import tpu as pltpu

D_MODEL = 1024
D_FF = 2816
CONV_WIDTH = D_MODEL
CONV_K = 3
N_Q_HEADS = 16
N_KV_HEADS = 2
HEAD_DIM = 64
GROUP = N_Q_HEADS // N_KV_HEADS
ATTN_WIDTH = N_Q_HEADS * HEAD_DIM
KV_WIDTH = N_KV_HEADS * HEAD_DIM
BLOCK = 128
FFN_RESIDUAL = 0.5
EPS = 1e-6
NEG_INF = -1e30

LANES = 128
SUBLANES = 8
PAIRS = GROUP // 2
VMEM_LIMIT_BYTES = 56 * 1024 * 1024

FFN_TILE = 512
FFN_CHUNKS = (768, 768, 768, 512)
MIX_TILE = 512


def _rms_scale(xf):
    return lax.rsqrt(jnp.mean(xf * xf, axis=-1, keepdims=True) + EPS)


def _dot(a, b):
    return jnp.dot(a, b, preferred_element_type=jnp.float32)


def _dot_nt(a, b):
    return lax.dot_general(a, b, (((1,), (1,)), ((), ())), preferred_element_type=jnp.float32)


def _ffn_kernel(x_ref, nw_ref, wg_ref, wu_ref, wd_ref, o_ref):
    x = x_ref[...]
    h = (x * _rms_scale(x) * nw_ref[...]).astype(jnp.bfloat16)
    acc = None
    c0 = 0
    for cw in FFN_CHUNKS:
        g = _dot(h, wg_ref[:, c0:c0 + cw])
        u = _dot(h, wu_ref[:, c0:c0 + cw])
        a = (jax.nn.silu(g) * u).astype(jnp.bfloat16)
        d = _dot(a, wd_ref[c0:c0 + cw, :])
        acc = d if acc is None else acc + d
        c0 += cw
    o_ref[...] = x + FFN_RESIDUAL * acc


def _const_spec(shape):
    return pl.BlockSpec(shape, lambda *_: (0,) * len(shape), pipeline_mode=pl.Buffered(1))


def _ffn(x2d, norm_w, w_gate, w_up, w_down):
    tokens = x2d.shape[0]
    assert tokens % FFN_TILE == 0 and sum(FFN_CHUNKS) == D_FF
    row_spec = pl.BlockSpec((FFN_TILE, D_MODEL), lambda i: (i, 0))
    return pl.pallas_call(
        _ffn_kernel,
        grid=(tokens // FFN_TILE,),
        in_specs=[
            row_spec,
            _const_spec((1, D_MODEL)),
            _const_spec((D_MODEL, D_FF)),
            _const_spec((D_MODEL, D_FF)),
            _const_spec((D_FF, D_MODEL)),
        ],
        out_specs=row_spec,
        out_shape=jax.ShapeDtypeStruct(x2d.shape, jnp.float32),
        compiler_params=pltpu.CompilerParams(
            dimension_semantics=("arbitrary",), vmem_limit_bytes=VMEM_LIMIT_BYTES),
        name="ffn",
    )(x2d, norm_w, w_gate, w_up, w_down)


def _mixer_kernel(sinks_ref, x_ref, nw_ref, wc_ref, wqkv_ref, convw_ref, qn_ref, kn_ref,
                  hsum_ref, wco_ref, wao_ref, wbg_ref, bbg_ref, wout_ref, o_ref,
                  zs_ref, q_ref, k_ref, v_ref, attn_ref):
    tq = MIX_TILE
    j = pl.program_id(1)
    x = x_ref[0]
    h = (x * _rms_scale(x) * nw_ref[...]).astype(jnp.bfloat16)

    @pl.when(j == 0)
    def _():
        zs_ref[0:SUBLANES, :] = jnp.zeros((SUBLANES, CONV_WIDTH), jnp.float32)
        k_ref[0:BLOCK, :] = jnp.zeros((BLOCK, KV_WIDTH), jnp.bfloat16)
        v_ref[0:BLOCK, :] = jnp.zeros((BLOCK, KV_WIDTH), jnp.bfloat16)

    pc = _dot(h, wc_ref[...])
    gb = pc[:, 0:CONV_WIDTH]
    z = pc[:, CONV_WIDTH:2 * CONV_WIDTH] * pc[:, 2 * CONV_WIDTH:3 * CONV_WIDTH]
    zs_ref[SUBLANES:SUBLANES + tq, :] = z
    cw = convw_ref[...]
    y = z * cw[CONV_K - 1:CONV_K, :]
    for tap in range(CONV_K - 1):
        back = CONV_K - 1 - tap
        y = y + zs_ref[SUBLANES - back:SUBLANES - back + tq, :] * cw[tap:tap + 1, :]
    zs_ref[0:SUBLANES, :] = zs_ref[tq:tq + SUBLANES, :]
    y_conv = _dot((gb * y).astype(jnp.bfloat16), wco_ref[...])

    gates = jax.nn.sigmoid(_dot(h, wbg_ref[...]) + bbg_ref[...])
    merged_c = gates[:, 0:D_MODEL] * y_conv

    qkv = _dot(h, wqkv_ref[...])
    q = qkv[:, 0:ATTN_WIDTH]
    k = qkv[:, ATTN_WIDTH:ATTN_WIDTH + KV_WIDTH]
    v = qkv[:, ATTN_WIDTH + KV_WIDTH:]
    hsum = hsum_ref[...]
    q_ms = _dot((q * q).astype(jnp.bfloat16), hsum) * (1.0 / HEAD_DIM)
    k_ms = _dot((k * k).astype(jnp.bfloat16), hsum[0:KV_WIDTH, 0:KV_WIDTH]) * (1.0 / HEAD_DIM)
    q_ref[...] = (q * lax.rsqrt(q_ms + EPS) * qn_ref[...]).astype(jnp.bfloat16)
    k_ref[BLOCK:BLOCK + tq, :] = (k * lax.rsqrt(k_ms + EPS) * kn_ref[...]).astype(jnp.bfloat16)
    v_ref[BLOCK:BLOCK + tq, :] = v.astype(jnp.bfloat16)

    lane = lax.broadcasted_iota(jnp.int32, (1, LANES), 1)
    lo_lanes = lane < HEAD_DIM
    rows = lax.broadcasted_iota(jnp.int32, (PAIRS * BLOCK, 1), 0)
    qpos = lax.broadcasted_iota(jnp.int32, (PAIRS * BLOCK, 2 * BLOCK), 0) % BLOCK
    kpos = lax.broadcasted_iota(jnp.int32, (PAIRS * BLOCK, 2 * BLOCK), 1)
    band = (kpos > qpos) & (kpos <= qpos + BLOCK)
    scale = HEAD_DIM ** -0.5

    for n in range(tq // BLOCK):
        r0 = n * BLOCK
        kw = k_ref[r0:r0 + 2 * BLOCK, :]
        vw = v_ref[r0:r0 + 2 * BLOCK, :]
        kw_sw = jnp.concatenate([kw[:, HEAD_DIM:], kw[:, :HEAD_DIM]], axis=1)
        vw_sw = jnp.concatenate([vw[:, HEAD_DIM:], vw[:, :HEAD_DIM]], axis=1)
        if n == 0:
            mask = band & ((kpos >= BLOCK) | (j > 0))
        else:
            mask = band
        zero = jnp.zeros_like(kw)
        for g in range(N_KV_HEADS):
            k_lo, k_hi = (kw, kw_sw) if g == 0 else (kw_sw, kw)
            v_lo, v_hi = (vw, vw_sw) if g == 0 else (vw_sw, vw)
            k_ext = jnp.concatenate([jnp.where(lo_lanes, k_lo, zero), jnp.where(lo_lanes, zero, k_hi)], axis=0)
            v_ext = jnp.concatenate([jnp.where(lo_lanes, v_lo, zero), jnp.where(lo_lanes, zero, v_hi)], axis=0)
            c0 = g * GROUP * HEAD_DIM
            qs = jnp.concatenate(
                [q_ref[r0:r0 + BLOCK, c0 + p * LANES:c0 + (p + 1) * LANES] for p in range(PAIRS)], axis=0)
            s = _dot_nt(qs, k_ext) * scale
            es = []
            inv = []
            for half in range(2):
                sink = jnp.zeros((PAIRS * BLOCK, 1), jnp.float32)
                for p in range(PAIRS):
                    head = g * GROUP + 2 * p + half
                    sink = jnp.where((rows >= p * BLOCK) & (rows < (p + 1) * BLOCK), sinks_ref[head], sink)
                sh = jnp.where(mask, s[:, half * 2 * BLOCK:(half + 1) * 2 * BLOCK], NEG_INF)
                m = jnp.maximum(jnp.max(sh, axis=-1, keepdims=True), sink)
                e = jnp.exp(sh - m)
                denom = jnp.sum(e, axis=-1, keepdims=True) + jnp.exp(sink - m)
                es.append(e.astype(jnp.bfloat16))
                inv.append(1.0 / denom)
            o = _dot(jnp.concatenate(es, axis=1), v_ext)
            o = o * jnp.where(lo_lanes, inv[0], inv[1])
            for p in range(PAIRS):
                attn_ref[r0:r0 + BLOCK, c0 + p * LANES:c0 + (p + 1) * LANES] = (
                    o[p * BLOCK:(p + 1) * BLOCK, :].astype(jnp.bfloat16))

    k_ref[0:BLOCK, :] = k_ref[tq:tq + BLOCK, :]
    v_ref[0:BLOCK, :] = v_ref[tq:tq + BLOCK, :]

    y_attn = _dot(attn_ref[...], wao_ref[...])
    merged = merged_c + gates[:, D_MODEL:] * y_attn
    o_ref[0] = x + _dot(merged.astype(jnp.bfloat16), wout_ref[...])


def _mixer(x, norm_w, w_conv_in, w_qkv, conv_w, q_norm, k_norm, sinks, w_conv_out, w_attn_out,
           w_branch_gate, b_branch_gate, w_out):
    batch, seq, _ = x.shape
    tq = MIX_TILE
    assert seq % tq == 0 and tq % BLOCK == 0
    head_id = jnp.arange(ATTN_WIDTH) // HEAD_DIM
    hsum = (head_id[:, None] == head_id[None, :]).astype(jnp.bfloat16)
    row_spec = pl.BlockSpec((1, tq, D_MODEL), lambda b, j: (b, j, 0))
    return pl.pallas_call(
        _mixer_kernel,
        grid=(batch, seq // tq),
        in_specs=[
            pl.BlockSpec(memory_space=pltpu.SMEM),
            row_spec,
            _const_spec((1, D_MODEL)),
            _const_spec((D_MODEL, 3 * CONV_WIDTH)),
            _const_spec((D_MODEL, ATTN_WIDTH + 2 * KV_WIDTH)),
            _const_spec((CONV_K, CONV_WIDTH)),
            _const_spec((1, ATTN_WIDTH)),
            _const_spec((1, KV_WIDTH)),
            _const_spec((ATTN_WIDTH, ATTN_WIDTH)),
            _const_spec((CONV_WIDTH, D_MODEL)),
            _const_spec((ATTN_WIDTH, D_MODEL)),
            _const_spec((D_MODEL, 2 * D_MODEL)),
            _const_spec((1, 2 * D_MODEL)),
            _const_spec((D_MODEL, D_MODEL)),
        ],
        out_specs=row_spec,
        out_shape=jax.ShapeDtypeStruct(x.shape, jnp.float32),
        scratch_shapes=[
            pltpu.VMEM((tq + SUBLANES, CONV_WIDTH), jnp.float32),
            pltpu.VMEM((tq, ATTN_WIDTH), jnp.bfloat16),
            pltpu.VMEM((tq + BLOCK, KV_WIDTH), jnp.bfloat16),
            pltpu.VMEM((tq + BLOCK, KV_WIDTH), jnp.bfloat16),
            pltpu.VMEM((tq, ATTN_WIDTH), jnp.bfloat16),
        ],
        compiler_params=pltpu.CompilerParams(
            dimension_semantics=("arbitrary", "arbitrary"), vmem_limit_bytes=VMEM_LIMIT_BYTES),
        name="mixer",
    )(sinks, x, norm_w, w_conv_in, w_qkv, conv_w, q_norm, k_norm, hsum, w_conv_out, w_attn_out,
      w_branch_gate, b_branch_gate, w_out)


def kernel(x, ffn1_norm, ffn1_w_gate, ffn1_w_up, ffn1_w_down, mix_norm, w_in, conv_w, q_norm, k_norm, sinks, w_conv_out, w_attn_out, w_branch_gate, b_branch_gate, w_out, ffn2_norm, ffn2_w_gate, ffn2_w_up, ffn2_w_down):
    batch, seq, d = x.shape
    depth = ffn1_norm.shape[0]
    bf = lambda w: w.astype(jnp.bfloat16)
    c0 = 3 * CONV_WIDTH
    for l in range(depth):
        x = _ffn(x.reshape(batch * seq, d), ffn1_norm[l][None], bf(ffn1_w_gate[l]), bf(ffn1_w_up[l]),
                 bf(ffn1_w_down[l])).reshape(batch, seq, d)
        x = _mixer(x, mix_norm[l][None], bf(w_in[l][:, :c0]), bf(w_in[l][:, c0:]), conv_w[l],
                   jnp.tile(q_norm[l], N_Q_HEADS)[None], jnp.tile(k_norm[l], N_KV_HEADS)[None], sinks[l],
                   bf(w_conv_out[l]), bf(w_attn_out[l]), bf(w_branch_gate[l]), b_branch_gate[l][None],
                   bf(w_out[l]))
        x = _ffn(x.reshape(batch * seq, d), ffn2_norm[l][None], bf(ffn2_w_gate[l]), bf(ffn2_w_up[l]),
                 bf(ffn2_w_down[l])).reshape(batch, seq, d)
    return x
```

```python
import math

import jax
import jax.numpy as jnp
from jax import lax
from jax.experimental import pallas as pl
from jax.experimental.pallas import tpu as pltpu

D_MODEL = 1024
D_FF = 2816
CONV_WIDTH = D_MODEL
CONV_K = 3
N_Q_HEADS = 16
N_KV_HEADS = 2
HEAD_DIM = 64
GROUP = N_Q_HEADS // N_KV_HEADS
ATTN_WIDTH = N_Q_HEADS * HEAD_DIM
KV_WIDTH = N_KV_HEADS * HEAD_DIM
BLOCK = 128
FFN_RESIDUAL = 0.5
EPS = 1e-6
NEG_INF = -1e30
LOG2E = math.log2(math.e)

LANES = 128
SUBLANES = 8
PAIRS = GROUP // 2
VMEM_LIMIT_BYTES = 56 * 1024 * 1024

FFN_TILE = 512
FFN_CHUNKS = (768, 768, 768, 512)
MIX_TILE = 512
SM_ROWS = 32


def _rms_scale(xf):
    return lax.rsqrt(jnp.mean(xf * xf, axis=-1, keepdims=True) + EPS)


def _dot(a, b):
    return jnp.dot(a, b, preferred_element_type=jnp.float32)


def _dot_nt(a, b):
    return lax.dot_general(a, b, (((1,), (1,)), ((), ())), preferred_element_type=jnp.float32)


def _ffn_kernel(x_ref, nw_ref, wg_ref, wu_ref, wd_ref, o_ref):
    x = x_ref[...]
    h = (x * _rms_scale(x) * nw_ref[...]).astype(jnp.bfloat16)
    acc = None
    c0 = 0
    for cw in FFN_CHUNKS:
        g = _dot(h, wg_ref[:, c0:c0 + cw])
        u = _dot(h, wu_ref[:, c0:c0 + cw])
        a = (jax.nn.silu(g) * u).astype(jnp.bfloat16)
        d = _dot(a, wd_ref[c0:c0 + cw, :])
        acc = d if acc is None else acc + d
        c0 += cw
    o_ref[...] = x + FFN_RESIDUAL * acc


def _const_spec(shape):
    return pl.BlockSpec(shape, lambda *_: (0,) * len(shape), pipeline_mode=pl.Buffered(1))


def _ffn(x2d, norm_w, w_gate, w_up, w_down):
    tokens = x2d.shape[0]
    assert tokens % FFN_TILE == 0 and sum(FFN_CHUNKS) == D_FF
    row_spec = pl.BlockSpec((FFN_TILE, D_MODEL), lambda i: (i, 0))
    return pl.pallas_call(
        _ffn_kernel,
        grid=(tokens // FFN_TILE,),
        in_specs=[
            row_spec,
            _const_spec((1, D_MODEL)),
            _const_spec((D_MODEL, D_FF)),
            _const_spec((D_MODEL, D_FF)),
            _const_spec((D_FF, D_MODEL)),
        ],
        out_specs=row_spec,
        out_shape=jax.ShapeDtypeStruct(x2d.shape, jnp.float32),
        compiler_params=pltpu.CompilerParams(
            dimension_semantics=("arbitrary",), vmem_limit_bytes=VMEM_LIMIT_BYTES),
        name="ffn",
    )(x2d, norm_w, w_gate, w_up, w_down)


def _mixer_kernel(sinks_ref, x_ref, nw_ref, wc_ref, wqkv_ref, convw_ref, qn_ref, kn_ref,
                  wco_ref, wao_ref, wbg_ref, bbg_ref, wout_ref, o_ref,
                  zs_ref, q_ref, k_ref, v_ref, e_ref, st_ref, attn_ref):
    tq = MIX_TILE
    j = pl.program_id(1)
    x = x_ref[0]
    h = (x * _rms_scale(x) * nw_ref[...]).astype(jnp.bfloat16)

    @pl.when(j == 0)
    def _():
        zs_ref[0:SUBLANES, :] = jnp.zeros((SUBLANES, CONV_WIDTH), jnp.float32)
        k_ref[0:BLOCK, :] = jnp.zeros((BLOCK, KV_WIDTH), jnp.bfloat16)
        v_ref[0:BLOCK, :] = jnp.zeros((BLOCK, KV_WIDTH), jnp.bfloat16)

    pc = _dot(h, wc_ref[...])
    gb = pc[:, 0:CONV_WIDTH]
    z = pc[:, CONV_WIDTH:2 * CONV_WIDTH] * pc[:, 2 * CONV_WIDTH:3 * CONV_WIDTH]
    zs_ref[SUBLANES:SUBLANES + tq, :] = z
    cw = convw_ref[...]
    y = z * cw[CONV_K - 1:CONV_K, :]
    for tap in range(CONV_K - 1):
        back = CONV_K - 1 - tap
        y = y + zs_ref[SUBLANES - back:SUBLANES - back + tq, :] * cw[tap:tap + 1, :]
    zs_ref[0:SUBLANES, :] = zs_ref[tq:tq + SUBLANES, :]
    y_conv = _dot((gb * y).astype(jnp.bfloat16), wco_ref[...])

    gates = jax.nn.sigmoid(_dot(h, wbg_ref[...]) + bbg_ref[...])
    merged_c = gates[:, 0:D_MODEL] * y_conv

    qkv = _dot(h, wqkv_ref[...])
    q = qkv[:, 0:ATTN_WIDTH]
    k = qkv[:, ATTN_WIDTH:ATTN_WIDTH + KV_WIDTH]
    v = qkv[:, ATTN_WIDTH + KV_WIDTH:]
    lane = lax.broadcasted_iota(jnp.int32, (1, LANES), 1)
    lo_lanes = lane < HEAD_DIM

    def head_pair_norm(col, gain):
        sq = col * col
        lo = jnp.sum(jnp.where(lo_lanes, sq, 0.0), axis=-1, keepdims=True)
        hi = jnp.sum(jnp.where(lo_lanes, 0.0, sq), axis=-1, keepdims=True)
        ms = jnp.where(lo_lanes, lo, hi) * (1.0 / HEAD_DIM)
        return (col * lax.rsqrt(ms + EPS) * gain).astype(jnp.bfloat16)

    qn = qn_ref[...]
    for p in range(ATTN_WIDTH // LANES):
        cols = slice(p * LANES, (p + 1) * LANES)
        q_ref[:, cols] = head_pair_norm(q[:, cols], qn[:, cols])
    k_ref[BLOCK:BLOCK + tq, :] = head_pair_norm(k, kn_ref[...])
    v_ref[BLOCK:BLOCK + tq, :] = v.astype(jnp.bfloat16)

    kl = lax.broadcasted_iota(jnp.int32, (SM_ROWS, LANES), 1)
    qi0 = lax.broadcasted_iota(jnp.int32, (SM_ROWS, LANES), 0)
    slab_row = lax.broadcasted_iota(jnp.int32, (2 * 2 * BLOCK, LANES), 0)
    ones_blk = jnp.where((slab_row < 2 * BLOCK) == lo_lanes, 1.0, 0.0).astype(jnp.bfloat16)

    for n in range(tq // BLOCK):
        r0 = n * BLOCK
        kw = k_ref[r0:r0 + 2 * BLOCK, :]
        vw = v_ref[r0:r0 + 2 * BLOCK, :]
        kw_sw = jnp.concatenate([kw[:, HEAD_DIM:], kw[:, :HEAD_DIM]], axis=1)
        vw_sw = jnp.concatenate([vw[:, HEAD_DIM:], vw[:, :HEAD_DIM]], axis=1)
        prev_off = jnp.where(j > 0, 0, BLOCK) if n == 0 else 0
        zero = jnp.zeros_like(kw)
        for g in range(N_KV_HEADS):
            k_lo, k_hi = (kw, kw_sw) if g == 0 else (kw_sw, kw)
            v_lo, v_hi = (vw, vw_sw) if g == 0 else (vw_sw, vw)
            k_ext = jnp.concatenate([jnp.where(lo_lanes, k_lo, zero), jnp.where(lo_lanes, zero, k_hi)], axis=0)
            v_ext = jnp.concatenate([jnp.where(lo_lanes, v_lo, zero), jnp.where(lo_lanes, zero, v_hi)], axis=0)
            v_ext = jnp.concatenate([v_ext, ones_blk], axis=1)
            c0 = g * GROUP * HEAD_DIM
            qs = jnp.concatenate(
                [q_ref[r0:r0 + BLOCK, c0 + p * LANES:c0 + (p + 1) * LANES] for p in range(PAIRS)], axis=0)
            s = _dot_nt(qs, k_ext)
            for t in range(PAIRS * BLOCK // SM_ROWS):
                rs = t * SM_ROWS
                qi = qi0 + rs % BLOCK
                prev_ok = kl > qi + prev_off
                cur_ok = kl <= qi
                sink_terms = []
                for half in range(2):
                    sink = sinks_ref[g * GROUP + 2 * (rs // BLOCK) + half]
                    cb = half * 2 * BLOCK
                    sp = jnp.where(prev_ok, s[rs:rs + SM_ROWS, cb:cb + BLOCK], NEG_INF)
                    sc = jnp.where(cur_ok, s[rs:rs + SM_ROWS, cb + BLOCK:cb + 2 * BLOCK], NEG_INF)
                    m = jnp.maximum(jnp.max(jnp.maximum(sp, sc), axis=-1, keepdims=True), sink)
                    e_ref[rs:rs + SM_ROWS, cb:cb + BLOCK] = jnp.exp2(sp - m).astype(jnp.bfloat16)
                    e_ref[rs:rs + SM_ROWS, cb + BLOCK:cb + 2 * BLOCK] = jnp.exp2(sc - m).astype(jnp.bfloat16)
                    sink_terms.append(jnp.exp2(sink - m))
                st_ref[rs:rs + SM_ROWS, :] = jnp.where(lo_lanes, sink_terms[0], sink_terms[1])
            o = _dot(e_ref[...], v_ext)
            o = o[:, 0:LANES] / (o[:, LANES:2 * LANES] + st_ref[...])
            for p in range(PAIRS):
                attn_ref[r0:r0 + BLOCK, c0 + p * LANES:c0 + (p + 1) * LANES] = (
                    o[p * BLOCK:(p + 1) * BLOCK, :].astype(jnp.bfloat16))

    k_ref[0:BLOCK, :] = k_ref[tq:tq + BLOCK, :]
    v_ref[0:BLOCK, :] = v_ref[tq:tq + BLOCK, :]

    y_attn = _dot(attn_ref[...], wao_ref[...])
    merged = merged_c + gates[:, D_MODEL:] * y_attn
    o_ref[0] = x + _dot(merged.astype(jnp.bfloat16), wout_ref[...])


def _mixer(x, norm_w, w_conv_in, w_qkv, conv_w, q_norm, k_norm, sinks, w_conv_out, w_attn_out,
           w_branch_gate, b_branch_gate, w_out):
    batch, seq, _ = x.shape
    tq = MIX_TILE
    assert seq % tq == 0 and tq % BLOCK == 0 and BLOCK % SM_ROWS == 0
    row_spec = pl.BlockSpec((1, tq, D_MODEL), lambda b, j: (b, j, 0))
    return pl.pallas_call(
        _mixer_kernel,
        grid=(batch, seq // tq),
        in_specs=[
            pl.BlockSpec(memory_space=pltpu.SMEM),
            row_spec,
            _const_spec((1, D_MODEL)),
            _const_spec((D_MODEL, 3 * CONV_WIDTH)),
            _const_spec((D_MODEL, ATTN_WIDTH + 2 * KV_WIDTH)),
            _const_spec((CONV_K, CONV_WIDTH)),
            _const_spec((1, ATTN_WIDTH)),
            _const_spec((1, KV_WIDTH)),
            _const_spec((CONV_WIDTH, D_MODEL)),
            _const_spec((ATTN_WIDTH, D_MODEL)),
            _const_spec((D_MODEL, 2 * D_MODEL)),
            _const_spec((1, 2 * D_MODEL)),
            _const_spec((D_MODEL, D_MODEL)),
        ],
        out_specs=row_spec,
        out_shape=jax.ShapeDtypeStruct(x.shape, jnp.float32),
        scratch_shapes=[
            pltpu.VMEM((tq + SUBLANES, CONV_WIDTH), jnp.float32),
            pltpu.VMEM((tq, ATTN_WIDTH), jnp.bfloat16),
            pltpu.VMEM((tq + BLOCK, KV_WIDTH), jnp.bfloat16),
            pltpu.VMEM((tq + BLOCK, KV_WIDTH), jnp.bfloat16),
            pltpu.VMEM((PAIRS * BLOCK, 4 * BLOCK), jnp.bfloat16),
            pltpu.VMEM((PAIRS * BLOCK, LANES), jnp.float32),
            pltpu.VMEM((tq, ATTN_WIDTH), jnp.bfloat16),
        ],
        compiler_params=pltpu.CompilerParams(
            dimension_semantics=("arbitrary", "arbitrary"), vmem_limit_bytes=VMEM_LIMIT_BYTES),
        name="mixer",
    )(sinks, x, norm_w, w_conv_in, w_qkv, conv_w, q_norm, k_norm, w_conv_out, w_attn_out,
      w_branch_gate, b_branch_gate, w_out)


def kernel(x, ffn1_norm, ffn1_w_gate, ffn1_w_up, ffn1_w_down, mix_norm, w_in, conv_w, q_norm, k_norm, sinks, w_conv_out, w_attn_out, w_branch_gate, b_branch_gate, w_out, ffn2_norm, ffn2_w_gate, ffn2_w_up, ffn2_w_down):
    batch, seq, d = x.shape
    depth = ffn1_norm.shape[0]
    bf = lambda w: w.astype(jnp.bfloat16)
    c0 = 3 * CONV_WIDTH
    q_gain_scale = HEAD_DIM ** -0.5 * LOG2E
    for l in range(depth):
        x = _ffn(x.reshape(batch * seq, d), ffn1_norm[l][None], bf(ffn1_w_gate[l]), bf(ffn1_w_up[l]),
                 bf(ffn1_w_down[l])).reshape(batch, seq, d)
        x = _mixer(x, mix_norm[l][None], bf(w_in[l][:, :c0]), bf(w_in[l][:, c0:]), conv_w[l],
                   jnp.tile(q_norm[l] * q_gain_scale, N_Q_HEADS)[None],
                   jnp.tile(k_norm[l], N_KV_HEADS)[None], sinks[l] * LOG2E,
                   bf(w_conv_out[l]), bf(w_attn_out[l]), bf(w_branch_gate[l]), b_branch_gate[l][None],
                   bf(w_out[l]))
        x = _ffn(x.reshape(batch * seq, d), ffn2_norm[l][None], bf(ffn2_w_gate[l]), bf(ffn2_w_up[l]),
                 bf(ffn2_w_down[l])).reshape(batch, seq, d)
    return x
```

```python
import math

import jax
import jax.numpy as jnp
from jax import lax
from jax.experimental import pallas as pl
from jax.experimental.pallas import tpu as pltpu

D_MODEL = 1024
D_FF = 2816
CONV_WIDTH = D_MODEL
CONV_K = 3
N_Q_HEADS = 16
N_KV_HEADS = 2
HEAD_DIM = 64
GROUP = N_Q_HEADS // N_KV_HEADS
ATTN_WIDTH = N_Q_HEADS * HEAD_DIM
KV_WIDTH = N_KV_HEADS * HEAD_DIM
BLOCK = 128
FFN_RESIDUAL = 0.5
EPS = 1e-6
NEG_INF = -1e30
LOG2E = math.log2(math.e)

LANES = 128
SUBLANES = 8
PAIRS = GROUP // 2
VMEM_LIMIT_BYTES = 56 * 1024 * 1024

FFN_TILE = 1024
FFN_SUBTILE = 256
FFN_CHUNKS = (768, 768, 768, 512)
MIX_TILE = 512
SM_ROWS = 32
CONV_BLOCK = 256
GATE_CHUNK = 2 * D_MODEL // (CONV_WIDTH // CONV_BLOCK)


def _rms_scale(xf):
    return lax.rsqrt(jnp.mean(xf * xf, axis=-1, keepdims=True) + EPS)


def _dot(a, b):
    return jnp.dot(a, b, preferred_element_type=jnp.float32)


def _dot_nt(a, b):
    return lax.dot_general(a, b, (((1,), (1,)), ((), ())), preferred_element_type=jnp.float32)


def _ffn_kernel(x_ref, nw_ref, wg_ref, wu_ref, wd_ref, o_ref):
    for r0 in range(0, FFN_TILE, FFN_SUBTILE):
        x = x_ref[r0:r0 + FFN_SUBTILE, :]
        h = (x * _rms_scale(x) * nw_ref[...]).astype(jnp.bfloat16)
        acc = None
        c0 = 0
        for cw in FFN_CHUNKS:
            g = _dot(h, wg_ref[:, c0:c0 + cw])
            u = _dot(h, wu_ref[:, c0:c0 + cw])
            a = (jax.nn.silu(g) * u).astype(jnp.bfloat16)
            d = _dot(a, wd_ref[c0:c0 + cw, :])
            acc = d if acc is None else acc + d
            c0 += cw
        o_ref[r0:r0 + FFN_SUBTILE, :] = x + FFN_RESIDUAL * acc


def _const_spec(shape):
    return pl.BlockSpec(shape, lambda *_: (0,) * len(shape), pipeline_mode=pl.Buffered(1))


def _ffn(x2d, norm_w, w_gate, w_up, w_down):
    tokens = x2d.shape[0]
    assert tokens % FFN_TILE == 0 and sum(FFN_CHUNKS) == D_FF
    row_spec = pl.BlockSpec((FFN_TILE, D_MODEL), lambda i: (i, 0))
    return pl.pallas_call(
        _ffn_kernel,
        grid=(tokens // FFN_TILE,),
        in_specs=[
            row_spec,
            _const_spec((1, D_MODEL)),
            _const_spec((D_MODEL, D_FF)),
            _const_spec((D_MODEL, D_FF)),
            _const_spec((D_FF, D_MODEL)),
        ],
        out_specs=row_spec,
        out_shape=jax.ShapeDtypeStruct(x2d.shape, jnp.float32),
        compiler_params=pltpu.CompilerParams(
            dimension_semantics=("arbitrary",), vmem_limit_bytes=VMEM_LIMIT_BYTES),
        name="ffn",
    )(x2d, norm_w, w_gate, w_up, w_down)


def _mixer_kernel(sinks_ref, x_ref, nw_ref, wc_ref, wqkv_ref, convw_ref, qn_ref, kn_ref,
                  wco_ref, wao_ref, wbg_ref, bbg_ref, wout_ref, o_ref,
                  zs_ref, act_ref, q_ref, k_ref, v_ref, e_ref, st_ref, attn_ref):
    tq = MIX_TILE
    j = pl.program_id(1)

    @pl.when(j == 0)
    def _():
        zs_ref[0:SUBLANES, :] = jnp.zeros((SUBLANES, CONV_WIDTH), jnp.float32)
        k_ref[0:BLOCK, :] = jnp.zeros((BLOCK, KV_WIDTH), jnp.bfloat16)
        v_ref[0:BLOCK, :] = jnp.zeros((BLOCK, KV_WIDTH), jnp.bfloat16)

    lane = lax.broadcasted_iota(jnp.int32, (1, LANES), 1)
    lo_lanes = lane < HEAD_DIM

    def head_pair_norm(col, gain):
        sq = col * col
        lo = jnp.sum(jnp.where(lo_lanes, sq, 0.0), axis=-1, keepdims=True)
        hi = jnp.sum(jnp.where(lo_lanes, 0.0, sq), axis=-1, keepdims=True)
        ms = jnp.where(lo_lanes, lo, hi) * (1.0 / HEAD_DIM)
        return (col * lax.rsqrt(ms + EPS) * gain).astype(jnp.bfloat16)

    qn = qn_ref[...]
    h_blocks = []
    for r0 in range(0, tq, BLOCK):
        xb = x_ref[0, r0:r0 + BLOCK, :]
        hb = (xb * _rms_scale(xb) * nw_ref[...]).astype(jnp.bfloat16)
        h_blocks.append(hb)
        qkv = _dot(hb, wqkv_ref[...])
        for p in range(ATTN_WIDTH // LANES):
            cols = slice(p * LANES, (p + 1) * LANES)
            q_ref[r0:r0 + BLOCK, cols] = head_pair_norm(qkv[:, cols], qn[:, cols])
        k_ref[BLOCK + r0:2 * BLOCK + r0, :] = head_pair_norm(qkv[:, ATTN_WIDTH:ATTN_WIDTH + KV_WIDTH], kn_ref[...])
        v_ref[BLOCK + r0:2 * BLOCK + r0, :] = qkv[:, ATTN_WIDTH + KV_WIDTH:].astype(jnp.bfloat16)
    h = jnp.concatenate(h_blocks, axis=0)

    def conv_channels(i):
        c = slice(i * CONV_BLOCK, (i + 1) * CONV_BLOCK)
        pc = _dot(h, wc_ref[:, 3 * i * CONV_BLOCK:3 * (i + 1) * CONV_BLOCK])
        gb = pc[:, 0:CONV_BLOCK]
        z = pc[:, CONV_BLOCK:2 * CONV_BLOCK] * pc[:, 2 * CONV_BLOCK:3 * CONV_BLOCK]
        zs_ref[SUBLANES:SUBLANES + tq, c] = z
        y = z * convw_ref[CONV_K - 1:CONV_K, c]
        for tap in range(CONV_K - 1):
            back = CONV_K - 1 - tap
            y = y + zs_ref[SUBLANES - back:SUBLANES - back + tq, c] * convw_ref[tap:tap + 1, c]
        zs_ref[0:SUBLANES, c] = zs_ref[tq:tq + SUBLANES, c]
        act_ref[:, c] = (gb * y).astype(jnp.bfloat16)

    gate_chunks = []
    y_conv_chunks = []

    def gates_and_conv_out(i):
        gc = slice(i * GATE_CHUNK, (i + 1) * GATE_CHUNK)
        gate_chunks.append(jax.nn.sigmoid(_dot(h, wbg_ref[:, gc]) + bbg_ref[:, gc]))
        c = slice(i * CONV_BLOCK, (i + 1) * CONV_BLOCK)
        y_conv_chunks.append(_dot(act_ref[...], wco_ref[:, c]))

    n_conv_blocks = CONV_WIDTH // CONV_BLOCK
    dense_pieces = ([lambda i=i: conv_channels(i) for i in range(n_conv_blocks)]
                    + [lambda i=i: gates_and_conv_out(i) for i in range(n_conv_blocks)])

    kl = lax.broadcasted_iota(jnp.int32, (SM_ROWS, LANES), 1)
    qi0 = lax.broadcasted_iota(jnp.int32, (SM_ROWS, LANES), 0)
    slab_row = lax.broadcasted_iota(jnp.int32, (2 * 2 * BLOCK, LANES), 0)
    ones_blk = jnp.where((slab_row < 2 * BLOCK) == lo_lanes, 1.0, 0.0).astype(jnp.bfloat16)

    def pair_slab(ref, r0, g, extra=None):
        w = ref[r0:r0 + 2 * BLOCK, :]
        w_sw = jnp.concatenate([w[:, HEAD_DIM:], w[:, :HEAD_DIM]], axis=1)
        lo, hi = (w, w_sw) if g == 0 else (w_sw, w)
        zero = jnp.zeros_like(w)
        slab = jnp.concatenate([jnp.where(lo_lanes, lo, zero), jnp.where(lo_lanes, zero, hi)], axis=0)
        return slab if extra is None else jnp.concatenate([slab, extra], axis=1)

    def scores(n, g):
        r0, c0 = n * BLOCK, g * GROUP * HEAD_DIM
        qs = jnp.concatenate(
            [q_ref[r0:r0 + BLOCK, c0 + p * LANES:c0 + (p + 1) * LANES] for p in range(PAIRS)], axis=0)
        return _dot_nt(qs, pair_slab(k_ref, r0, g))

    def softmax(n, g, s, slot):
        prev_off = jnp.where(j > 0, 0, BLOCK) if n == 0 else 0
        for t in range(PAIRS * BLOCK // SM_ROWS):
            rs = t * SM_ROWS
            qi = qi0 + rs % BLOCK
            prev_ok = kl > qi + prev_off
            cur_ok = kl <= qi
            sink_terms = []
            for half in range(2):
                sink = sinks_ref[g * GROUP + 2 * (rs // BLOCK) + half]
                cb = half * 2 * BLOCK
                sp = jnp.where(prev_ok, s[rs:rs + SM_ROWS, cb:cb + BLOCK], NEG_INF)
                sc = jnp.where(cur_ok, s[rs:rs + SM_ROWS, cb + BLOCK:cb + 2 * BLOCK], NEG_INF)
                m = jnp.maximum(jnp.max(jnp.maximum(sp, sc), axis=-1, keepdims=True), sink)
                e_ref[slot, rs:rs + SM_ROWS, cb:cb + BLOCK] = jnp.exp2(sp - m).astype(jnp.bfloat16)
                e_ref[slot, rs:rs + SM_ROWS, cb + BLOCK:cb + 2 * BLOCK] = jnp.exp2(sc - m).astype(jnp.bfloat16)
                sink_terms.append(jnp.exp2(sink - m))
            st_ref[slot, rs:rs + SM_ROWS, :] = jnp.where(lo_lanes, sink_terms[0], sink_terms[1])

    def weighted_values(n, g, slot):
        r0, c0 = n * BLOCK, g * GROUP * HEAD_DIM
        o = _dot(e_ref[slot], pair_slab(v_ref, r0, g, ones_blk))
        o = o[:, 0:LANES] / (o[:, LANES:2 * LANES] + st_ref[slot])
        for p in range(PAIRS):
            attn_ref[r0:r0 + BLOCK, c0 + p * LANES:c0 + (p + 1) * LANES] = (
                o[p * BLOCK:(p + 1) * BLOCK, :].astype(jnp.bfloat16))

    steps = [(n, g) for n in range(tq // BLOCK) for g in range(N_KV_HEADS)]
    assert len(dense_pieces) == len(steps)
    s_next = scores(*steps[0])
    for i, (n, g) in enumerate(steps):
        s = s_next
        if i + 1 < len(steps):
            s_next = scores(*steps[i + 1])
        dense_pieces[i]()
        softmax(n, g, s, i % 2)
        weighted_values(n, g, i % 2)
    gates = jnp.concatenate(gate_chunks, axis=1)
    y_conv = jnp.concatenate(y_conv_chunks, axis=1)

    k_ref[0:BLOCK, :] = k_ref[tq:tq + BLOCK, :]
    v_ref[0:BLOCK, :] = v_ref[tq:tq + BLOCK, :]

    y_attn = _dot(attn_ref[...], wao_ref[...])
    merged = gates[:, 0:D_MODEL] * y_conv + gates[:, D_MODEL:] * y_attn
    o_ref[0] = x_ref[0] + _dot(merged.astype(jnp.bfloat16), wout_ref[...])


def _mixer(x, norm_w, w_conv_in, w_qkv, conv_w, q_norm, k_norm, sinks, w_conv_out, w_attn_out,
           w_branch_gate, b_branch_gate, w_out):
    batch, seq, _ = x.shape
    tq = MIX_TILE
    assert seq % tq == 0 and tq % BLOCK == 0 and BLOCK % SM_ROWS == 0
    row_spec = pl.BlockSpec((1, tq, D_MODEL), lambda b, j: (b, j, 0))
    return pl.pallas_call(
        _mixer_kernel,
        grid=(batch, seq // tq),
        in_specs=[
            pl.BlockSpec(memory_space=pltpu.SMEM),
            row_spec,
            _const_spec((1, D_MODEL)),
            _const_spec((D_MODEL, 3 * CONV_WIDTH)),
            _const_spec((D_MODEL, ATTN_WIDTH + 2 * KV_WIDTH)),
            _const_spec((CONV_K, CONV_WIDTH)),
            _const_spec((1, ATTN_WIDTH)),
            _const_spec((1, KV_WIDTH)),
            _const_spec((CONV_WIDTH, D_MODEL)),
            _const_spec((ATTN_WIDTH, D_MODEL)),
            _const_spec((D_MODEL, 2 * D_MODEL)),
            _const_spec((1, 2 * D_MODEL)),
            _const_spec((D_MODEL, D_MODEL)),
        ],
        out_specs=row_spec,
        out_shape=jax.ShapeDtypeStruct(x.shape, jnp.float32),
        scratch_shapes=[
            pltpu.VMEM((tq + SUBLANES, CONV_WIDTH), jnp.float32),
            pltpu.VMEM((tq, CONV_WIDTH), jnp.bfloat16),
            pltpu.VMEM((tq, ATTN_WIDTH), jnp.bfloat16),
            pltpu.VMEM((tq + BLOCK, KV_WIDTH), jnp.bfloat16),
            pltpu.VMEM((tq + BLOCK, KV_WIDTH), jnp.bfloat16),
            pltpu.VMEM((2, PAIRS * BLOCK, 4 * BLOCK), jnp.bfloat16),
            pltpu.VMEM((2, PAIRS * BLOCK, LANES), jnp.float32),
            pltpu.VMEM((tq, ATTN_WIDTH), jnp.bfloat16),
        ],
        compiler_params=pltpu.CompilerParams(
            dimension_semantics=("arbitrary", "arbitrary"), vmem_limit_bytes=VMEM_LIMIT_BYTES),
        name="mixer",
    )(sinks, x, norm_w, w_conv_in, w_qkv, conv_w, q_norm, k_norm, w_conv_out, w_attn_out,
      w_branch_gate, b_branch_gate, w_out)


def kernel(x, ffn1_norm, ffn1_w_gate, ffn1_w_up, ffn1_w_down, mix_norm, w_in, conv_w, q_norm, k_norm, sinks, w_conv_out, w_attn_out, w_branch_gate, b_branch_gate, w_out, ffn2_norm, ffn2_w_gate, ffn2_w_up, ffn2_w_down):
    batch, seq, d = x.shape
    depth = ffn1_norm.shape[0]
    bf = lambda w: w.astype(jnp.bfloat16)
    c0 = 3 * CONV_WIDTH
    q_gain_scale = HEAD_DIM ** -0.5 * LOG2E
    n_blk = CONV_WIDTH // CONV_BLOCK
    for l in range(depth):
        x = _ffn(x.reshape(batch * seq, d), ffn1_norm[l][None], bf(ffn1_w_gate[l]), bf(ffn1_w_up[l]),
                 bf(ffn1_w_down[l])).reshape(batch, seq, d)
        w_conv_in = bf(w_in[l][:, :c0]).reshape(d, 3, n_blk, CONV_BLOCK).transpose(0, 2, 1, 3).reshape(d, c0)
        x = _mixer(x, mix_norm[l][None], w_conv_in, bf(w_in[l][:, c0:]), conv_w[l],
                   jnp.tile(q_norm[l] * q_gain_scale, N_Q_HEADS)[None],
                   jnp.tile(k_norm[l], N_KV_HEADS)[None], sinks[l] * LOG2E,
                   bf(w_conv_out[l]), bf(w_attn_out[l]), bf(w_branch_gate[l]), b_branch_gate[l][None],
                   bf(w_out[l]))
        x = _ffn(x.reshape(batch * seq, d), ffn2_norm[l][None], bf(ffn2_w_gate[l]), bf(ffn2_w_up[l]),
                 bf(ffn2_w_down[l])).reshape(batch, seq, d)
    return x
```

```python
import functools
import math

import jax
import jax.numpy as jnp
from jax import lax
from jax.experimental import pallas as pl
from jax.experimental.pallas import tpu as pltpu

D_MODEL = 1024
D_FF = 2816
CONV_WIDTH = D_MODEL
CONV_K = 3
N_Q_HEADS = 16
N_KV_HEADS = 2
HEAD_DIM = 64
GROUP = N_Q_HEADS // N_KV_HEADS
ATTN_WIDTH = N_Q_HEADS * HEAD_DIM
KV_WIDTH = N_KV_HEADS * HEAD_DIM
BLOCK = 128
FFN_RESIDUAL = 0.5
EPS = 1e-6
NEG_INF = -1e30
LOG2E = math.log2(math.e)

LANES = 128
SUBLANES = 8
PAIRS = GROUP // 2
VMEM_LIMIT_BYTES = 56 * 1024 * 1024

FFN_TILE = 1024
FFN_SUBTILE = 256
FFN_CHUNKS = (768, 768, 768, 512)
MIX_TILE = 512
SM_ROWS = 32
SCORE_BOUND_SLACK = 1.02
SCORE_BOUND_MAX = 30.0
CONV_BLOCK = 256
GATE_CHUNK = 2 * D_MODEL // (CONV_WIDTH // CONV_BLOCK)


def _rms_scale(xf):
    return lax.rsqrt(jnp.mean(xf * xf, axis=-1, keepdims=True) + EPS)


def _dot(a, b):
    return jnp.dot(a, b, preferred_element_type=jnp.float32)


def _dot_nt(a, b):
    return lax.dot_general(a, b, (((1,), (1,)), ((), ())), preferred_element_type=jnp.float32)


def _ffn_kernel(x_ref, nw_ref, wg_ref, wu_ref, wd_ref, o_ref):
    for r0 in range(0, FFN_TILE, FFN_SUBTILE):
        x = x_ref[r0:r0 + FFN_SUBTILE, :]
        h = (x * _rms_scale(x) * nw_ref[...]).astype(jnp.bfloat16)
        acc = None
        c0 = 0
        for cw in FFN_CHUNKS:
            g = _dot(h, wg_ref[:, c0:c0 + cw])
            u = _dot(h, wu_ref[:, c0:c0 + cw])
            a = (jax.nn.silu(g) * u).astype(jnp.bfloat16)
            d = _dot(a, wd_ref[c0:c0 + cw, :])
            acc = d if acc is None else acc + d
            c0 += cw
        o_ref[r0:r0 + FFN_SUBTILE, :] = x + FFN_RESIDUAL * acc


def _const_spec(shape):
    return pl.BlockSpec(shape, lambda *_: (0,) * len(shape), pipeline_mode=pl.Buffered(1))


def _ffn(x2d, norm_w, w_gate, w_up, w_down):
    tokens = x2d.shape[0]
    assert tokens % FFN_TILE == 0 and sum(FFN_CHUNKS) == D_FF
    row_spec = pl.BlockSpec((FFN_TILE, D_MODEL), lambda i: (i, 0))
    return pl.pallas_call(
        _ffn_kernel,
        grid=(tokens // FFN_TILE,),
        in_specs=[
            row_spec,
            _const_spec((1, D_MODEL)),
            _const_spec((D_MODEL, D_FF)),
            _const_spec((D_MODEL, D_FF)),
            _const_spec((D_FF, D_MODEL)),
        ],
        out_specs=row_spec,
        out_shape=jax.ShapeDtypeStruct(x2d.shape, jnp.float32),
        compiler_params=pltpu.CompilerParams(
            dimension_semantics=("arbitrary",), vmem_limit_bytes=VMEM_LIMIT_BYTES),
        name="ffn",
    )(x2d, norm_w, w_gate, w_up, w_down)


def _mixer_kernel(sm_ref, x_ref, nw_ref, win_ref, convw_ref, qn_ref, kn_ref,
                  wco_ref, wao_ref, wbg_ref, bbg_ref, wout_ref, o_ref,
                  zs_ref, act_ref, q_ref, k_ref, v_ref, e_ref, st_ref, attn_ref, *, bounded_scores):
    tq = MIX_TILE
    j = pl.program_id(1)

    @pl.when(j == 0)
    def _():
        zs_ref[0:SUBLANES, :] = jnp.zeros((SUBLANES, CONV_WIDTH), jnp.float32)
        k_ref[0:BLOCK, :] = jnp.zeros((BLOCK, KV_WIDTH), jnp.bfloat16)
        v_ref[0:BLOCK, :] = jnp.zeros((BLOCK, KV_WIDTH), jnp.bfloat16)

    lane = lax.broadcasted_iota(jnp.int32, (1, LANES), 1)
    lo_lanes = lane < HEAD_DIM

    def head_pair_norm(col, gain):
        sq = col * col
        lo = jnp.sum(jnp.where(lo_lanes, sq, 0.0), axis=-1, keepdims=True)
        hi = jnp.sum(jnp.where(lo_lanes, 0.0, sq), axis=-1, keepdims=True)
        ms = jnp.where(lo_lanes, lo, hi) * (1.0 / HEAD_DIM)
        return (col * lax.rsqrt(ms + EPS) * gain).astype(jnp.bfloat16)

    qn = qn_ref[...]
    h_blocks = []
    for r0 in range(0, tq, BLOCK):
        xb = x_ref[0, r0:r0 + BLOCK, :]
        hb = (xb * _rms_scale(xb) * nw_ref[...]).astype(jnp.bfloat16)
        h_blocks.append(hb)
        qkv = _dot(hb, win_ref[:, 3 * CONV_WIDTH:])
        for p in range(ATTN_WIDTH // LANES):
            cols = slice(p * LANES, (p + 1) * LANES)
            q_ref[r0:r0 + BLOCK, cols] = head_pair_norm(qkv[:, cols], qn[:, cols])
        k_ref[BLOCK + r0:2 * BLOCK + r0, :] = head_pair_norm(qkv[:, ATTN_WIDTH:ATTN_WIDTH + KV_WIDTH], kn_ref[...])
        v_ref[BLOCK + r0:2 * BLOCK + r0, :] = qkv[:, ATTN_WIDTH + KV_WIDTH:].astype(jnp.bfloat16)
    h = jnp.concatenate(h_blocks, axis=0)

    def conv_channels(i):
        c = slice(i * CONV_BLOCK, (i + 1) * CONV_BLOCK)
        gb, gc, vc = (_dot(h, win_ref[:, part * CONV_WIDTH + i * CONV_BLOCK:part * CONV_WIDTH + (i + 1) * CONV_BLOCK])
                      for part in range(3))
        z = gc * vc
        zs_ref[SUBLANES:SUBLANES + tq, c] = z
        y = z * convw_ref[CONV_K - 1:CONV_K, c]
        for tap in range(CONV_K - 1):
            back = CONV_K - 1 - tap
            y = y + zs_ref[SUBLANES - back:SUBLANES - back + tq, c] * convw_ref[tap:tap + 1, c]
        zs_ref[0:SUBLANES, c] = zs_ref[tq:tq + SUBLANES, c]
        act_ref[:, c] = (gb * y).astype(jnp.bfloat16)

    gate_chunks = []
    y_conv_chunks = []

    def gates_and_conv_out(i):
        gc = slice(i * GATE_CHUNK, (i + 1) * GATE_CHUNK)
        gate_chunks.append(jax.nn.sigmoid(_dot(h, wbg_ref[:, gc]) + bbg_ref[:, gc]))
        c = slice(i * CONV_BLOCK, (i + 1) * CONV_BLOCK)
        y_conv_chunks.append(_dot(act_ref[...], wco_ref[:, c]))

    n_conv_blocks = CONV_WIDTH // CONV_BLOCK
    dense_pieces = ([lambda i=i: conv_channels(i) for i in range(n_conv_blocks)]
                    + [lambda i=i: gates_and_conv_out(i) for i in range(n_conv_blocks)])

    kl = lax.broadcasted_iota(jnp.int32, (SM_ROWS, LANES), 1)
    qi0 = lax.broadcasted_iota(jnp.int32, (SM_ROWS, LANES), 0)
    slab_row = lax.broadcasted_iota(jnp.int32, (2 * 2 * BLOCK, LANES), 0)
    ones_blk = jnp.where((slab_row < 2 * BLOCK) == lo_lanes, 1.0, 0.0).astype(jnp.bfloat16)

    def pair_slab(ref, r0, g, extra=None):
        w = ref[r0:r0 + 2 * BLOCK, :]
        w_sw = jnp.concatenate([w[:, HEAD_DIM:], w[:, :HEAD_DIM]], axis=1)
        lo, hi = (w, w_sw) if g == 0 else (w_sw, w)
        zero = jnp.zeros_like(w)
        slab = jnp.concatenate([jnp.where(lo_lanes, lo, zero), jnp.where(lo_lanes, zero, hi)], axis=0)
        return slab if extra is None else jnp.concatenate([slab, extra], axis=1)

    def scores(n, g):
        r0, c0 = n * BLOCK, g * GROUP * HEAD_DIM
        qs = jnp.concatenate(
            [q_ref[r0:r0 + BLOCK, c0 + p * LANES:c0 + (p + 1) * LANES] for p in range(PAIRS)], axis=0)
        return _dot_nt(qs, pair_slab(k_ref, r0, g))

    def softmax(n, g, s, slot):
        prev_off = jnp.where(j > 0, 0, BLOCK) if n == 0 else 0
        for t in range(PAIRS * BLOCK // SM_ROWS):
            rs = t * SM_ROWS
            qi = qi0 + rs % BLOCK
            prev_ok = kl > qi + prev_off
            cur_ok = kl <= qi
            sink_terms = []
            for half in range(2):
                cb = half * 2 * BLOCK
                sp = jnp.where(prev_ok, s[rs:rs + SM_ROWS, cb:cb + BLOCK], NEG_INF)
                sc = jnp.where(cur_ok, s[rs:rs + SM_ROWS, cb + BLOCK:cb + 2 * BLOCK], NEG_INF)
                if bounded_scores:
                    m = sm_ref[2 * N_Q_HEADS]
                else:
                    sink = sm_ref[g * GROUP + 2 * (rs // BLOCK) + half]
                    m = jnp.maximum(jnp.max(jnp.maximum(sp, sc), axis=-1, keepdims=True), sink)
                    sink_terms.append(jnp.exp2(sink - m))
                e_ref[slot, rs:rs + SM_ROWS, cb:cb + BLOCK] = jnp.exp2(sp - m).astype(jnp.bfloat16)
                e_ref[slot, rs:rs + SM_ROWS, cb + BLOCK:cb + 2 * BLOCK] = jnp.exp2(sc - m).astype(jnp.bfloat16)
            if not bounded_scores:
                st_ref[slot, rs:rs + SM_ROWS, :] = jnp.where(lo_lanes, sink_terms[0], sink_terms[1])

    def weighted_values(n, g, slot):
        r0, c0 = n * BLOCK, g * GROUP * HEAD_DIM
        o = _dot(e_ref[slot], pair_slab(v_ref, r0, g, ones_blk))
        for p in range(PAIRS):
            rows = slice(p * BLOCK, (p + 1) * BLOCK)
            if bounded_scores:
                head = N_Q_HEADS + g * GROUP + 2 * p
                sink_term = jnp.where(lo_lanes, sm_ref[head], sm_ref[head + 1])
            else:
                sink_term = st_ref[slot, rows, :]
            op = o[rows, 0:LANES] / (o[rows, LANES:2 * LANES] + sink_term)
            attn_ref[r0:r0 + BLOCK, c0 + p * LANES:c0 + (p + 1) * LANES] = op.astype(jnp.bfloat16)

    steps = [(n, g) for n in range(tq // BLOCK) for g in range(N_KV_HEADS)]
    assert len(dense_pieces) == len(steps)
    s_next = scores(*steps[0])
    for i, (n, g) in enumerate(steps):
        s = s_next
        if i + 1 < len(steps):
            s_next = scores(*steps[i + 1])
        dense_pieces[i]()
        softmax(n, g, s, i % 2)
        weighted_values(n, g, i % 2)
    gates = jnp.concatenate(gate_chunks, axis=1)
    y_conv = jnp.concatenate(y_conv_chunks, axis=1)

    k_ref[0:BLOCK, :] = k_ref[tq:tq + BLOCK, :]
    v_ref[0:BLOCK, :] = v_ref[tq:tq + BLOCK, :]

    y_attn = _dot(attn_ref[...], wao_ref[...])
    merged = gates[:, 0:D_MODEL] * y_conv + gates[:, D_MODEL:] * y_attn
    o_ref[0] = x_ref[0] + _dot(merged.astype(jnp.bfloat16), wout_ref[...])


def _mixer(bounded_scores, sm, x, norm_w, w_in, conv_w, q_gain, k_gain, w_conv_out, w_attn_out,
           w_branch_gate, b_branch_gate, w_out):
    batch, seq, _ = x.shape
    tq = MIX_TILE
    assert seq % tq == 0 and tq % BLOCK == 0 and BLOCK % SM_ROWS == 0
    row_spec = pl.BlockSpec((1, tq, D_MODEL), lambda b, j: (b, j, 0))
    return pl.pallas_call(
        functools.partial(_mixer_kernel, bounded_scores=bounded_scores),
        grid=(batch, seq // tq),
        in_specs=[
            pl.BlockSpec(memory_space=pltpu.SMEM),
            row_spec,
            _const_spec((1, D_MODEL)),
            _const_spec((D_MODEL, 3 * CONV_WIDTH + ATTN_WIDTH + 2 * KV_WIDTH)),
            _const_spec((CONV_K, CONV_WIDTH)),
            _const_spec((1, ATTN_WIDTH)),
            _const_spec((1, KV_WIDTH)),
            _const_spec((CONV_WIDTH, D_MODEL)),
            _const_spec((ATTN_WIDTH, D_MODEL)),
            _const_spec((D_MODEL, 2 * D_MODEL)),
            _const_spec((1, 2 * D_MODEL)),
            _const_spec((D_MODEL, D_MODEL)),
        ],
        out_specs=row_spec,
        out_shape=jax.ShapeDtypeStruct(x.shape, jnp.float32),
        scratch_shapes=[
            pltpu.VMEM((tq + SUBLANES, CONV_WIDTH), jnp.float32),
            pltpu.VMEM((tq, CONV_WIDTH), jnp.bfloat16),
            pltpu.VMEM((tq, ATTN_WIDTH), jnp.bfloat16),
            pltpu.VMEM((tq + BLOCK, KV_WIDTH), jnp.bfloat16),
            pltpu.VMEM((tq + BLOCK, KV_WIDTH), jnp.bfloat16),
            pltpu.VMEM((2, PAIRS * BLOCK, 4 * BLOCK), jnp.bfloat16),
            pltpu.VMEM((2, PAIRS * BLOCK, LANES), jnp.float32),
            pltpu.VMEM((tq, ATTN_WIDTH), jnp.bfloat16),
        ],
        compiler_params=pltpu.CompilerParams(
            dimension_semantics=("arbitrary", "arbitrary"), vmem_limit_bytes=VMEM_LIMIT_BYTES),
        name="mixer_bounded" if bounded_scores else "mixer",
    )(sm, x, norm_w, w_in, conv_w, q_gain, k_gain, w_conv_out, w_attn_out,
      w_branch_gate, b_branch_gate, w_out)


def kernel(x, ffn1_norm, ffn1_w_gate, ffn1_w_up, ffn1_w_down, mix_norm, w_in, conv_w, q_norm, k_norm, sinks, w_conv_out, w_attn_out, w_branch_gate, b_branch_gate, w_out, ffn2_norm, ffn2_w_gate, ffn2_w_up, ffn2_w_down):
    batch, seq, d = x.shape
    depth = ffn1_norm.shape[0]
    bf = lambda w: w.astype(jnp.bfloat16)
    q_gain_scale = HEAD_DIM ** -0.5 * LOG2E
    for l in range(depth):
        x = _ffn(x.reshape(batch * seq, d), ffn1_norm[l][None], bf(ffn1_w_gate[l]), bf(ffn1_w_up[l]),
                 bf(ffn1_w_down[l])).reshape(batch, seq, d)

        q_gain = q_norm[l] * q_gain_scale
        sinks2 = sinks[l] * LOG2E
        bound = HEAD_DIM * jnp.max(jnp.abs(q_gain)) * jnp.max(jnp.abs(k_norm[l])) * SCORE_BOUND_SLACK
        sm = jnp.concatenate([sinks2, jnp.exp2(sinks2 - bound), bound[None]])
        mixer_args = (sm, x, mix_norm[l][None], bf(w_in[l]), conv_w[l],
                      jnp.tile(q_gain, N_Q_HEADS)[None], jnp.tile(k_norm[l], N_KV_HEADS)[None],
                      bf(w_conv_out[l]), bf(w_attn_out[l]), bf(w_branch_gate[l]), b_branch_gate[l][None],
                      bf(w_out[l]))
        x = lax.cond(bound <= SCORE_BOUND_MAX,
                     functools.partial(_mixer, True), functools.partial(_mixer, False), *mixer_args)

        x = _ffn(x.reshape(batch * seq, d), ffn2_norm[l][None], bf(ffn2_w_gate[l]), bf(ffn2_w_up[l]),
                 bf(ffn2_w_down[l])).reshape(batch, seq, d)
    return x
```

```python
import functools
import math

import jax
import jax.numpy as jnp
from jax import lax
from jax.experimental import pallas as pl
from jax.experimental.pallas import tpu as pltpu

D_MODEL = 1024
D_FF = 2816
CONV_WIDTH = D_MODEL
CONV_K = 3
N_Q_HEADS = 16
N_KV_HEADS = 2
HEAD_DIM = 64
GROUP = N_Q_HEADS // N_KV_HEADS
ATTN_WIDTH = N_Q_HEADS * HEAD_DIM
KV_WIDTH = N_KV_HEADS * HEAD_DIM
BLOCK = 128
FFN_RESIDUAL = 0.5
EPS = 1e-6
NEG_INF = -1e30
LOG2E = math.log2(math.e)

LANES = 128
SUBLANES = 8
BF16_ROWS = 2 * SUBLANES
PAIRS = GROUP // 2
VMEM_LIMIT_BYTES = 56 * 1024 * 1024

FFN_TILE = 1024
FFN_SUBTILE = 256
FFN_CHUNKS = (768, 768, 768, 512)
MIX_TILE = 512
SM_ROWS = 32
SCORE_BOUND_SLACK = 1.02
SCORE_BOUND_MAX = 30.0
CONV_BLOCK = 256
GATE_CHUNK = 2 * D_MODEL // (CONV_WIDTH // CONV_BLOCK)


def _rms_scale(xf):
    return lax.rsqrt(jnp.mean(xf * xf, axis=-1, keepdims=True) + EPS)


def _dot(a, b):
    return jnp.dot(a, b, preferred_element_type=jnp.float32)


def _dot_nt(a, b):
    return lax.dot_general(a, b, (((1,), (1,)), ((), ())), preferred_element_type=jnp.float32)


def _cast_plan(weight, n_steps):
    rows, cols = weight.shape
    rb = -(-rows // n_steps)
    rb = -(-rb // BF16_ROWS) * BF16_ROWS
    n_blocks, reps = rows // rb, n_steps * rb // rows
    assert rows % rb == 0 and reps >= 1
    return rb, cols, n_blocks, reps


def _cast_specs(weights, n_steps, step_of):
    specs, shapes = [], []
    for w in weights:
        rb, cols, n_blocks, reps = _cast_plan(w, n_steps)
        specs.append(pl.BlockSpec(
            (rb, cols),
            lambda *idx, reps=reps, last=n_blocks - 1: (jnp.minimum(step_of(*idx) // reps, last), 0)))
        shapes.append(jax.ShapeDtypeStruct(w.shape, jnp.bfloat16))
    return specs, shapes


def _cast_blocks(in_refs, out_refs):
    for src, dst in zip(in_refs, out_refs):
        dst[...] = src[...].astype(jnp.bfloat16)


def _ffn_kernel(x_ref, nw_ref, wg_ref, wu_ref, wd_ref, *rest):
    n_cast = len(rest) // 2
    o_ref = rest[n_cast]
    for r0 in range(0, FFN_TILE, FFN_SUBTILE):
        x = x_ref[r0:r0 + FFN_SUBTILE, :]
        h = (x * _rms_scale(x) * nw_ref[...]).astype(jnp.bfloat16)
        acc = None
        c0 = 0
        for cw in FFN_CHUNKS:
            g = _dot(h, wg_ref[:, c0:c0 + cw])
            u = _dot(h, wu_ref[:, c0:c0 + cw])
            a = (jax.nn.silu(g) * u).astype(jnp.bfloat16)
            d = _dot(a, wd_ref[c0:c0 + cw, :])
            acc = d if acc is None else acc + d
            c0 += cw
        o_ref[r0:r0 + FFN_SUBTILE, :] = x + FFN_RESIDUAL * acc
    _cast_blocks(rest[:n_cast], rest[n_cast + 1:])


def _const_spec(shape):
    return pl.BlockSpec(shape, lambda *_: (0,) * len(shape), pipeline_mode=pl.Buffered(1))


def _ffn(x2d, norm_w, w_gate, w_up, w_down, cast=()):
    tokens = x2d.shape[0]
    assert tokens % FFN_TILE == 0 and sum(FFN_CHUNKS) == D_FF
    n_steps = tokens // FFN_TILE
    row_spec = pl.BlockSpec((FFN_TILE, D_MODEL), lambda i: (i, 0))
    cast_specs, cast_shapes = _cast_specs(cast, n_steps, lambda i: i)
    return pl.pallas_call(
        _ffn_kernel,
        grid=(n_steps,),
        in_specs=[
            row_spec,
            _const_spec((1, D_MODEL)),
            _const_spec((D_MODEL, D_FF)),
            _const_spec((D_MODEL, D_FF)),
            _const_spec((D_FF, D_MODEL)),
            *cast_specs,
        ],
        out_specs=[row_spec, *cast_specs],
        out_shape=[jax.ShapeDtypeStruct(x2d.shape, jnp.float32), *cast_shapes],
        compiler_params=pltpu.CompilerParams(
            dimension_semantics=("arbitrary",), vmem_limit_bytes=VMEM_LIMIT_BYTES),
        name="ffn",
    )(x2d, norm_w, w_gate, w_up, w_down, *cast)


def _mixer_kernel(sm_ref, x_ref, nw_ref, win_ref, convw_ref, qn_ref, kn_ref,
                  wco_ref, wao_ref, wbg_ref, bbg_ref, wout_ref, *rest, bounded_scores, n_cast):
    cast_in, o_ref, cast_out = rest[:n_cast], rest[n_cast], rest[n_cast + 1:2 * n_cast + 1]
    zs_ref, act_ref, q_ref, k_ref, v_ref, e_ref, st_ref, attn_ref = rest[2 * n_cast + 1:]
    tq = MIX_TILE
    j = pl.program_id(1)

    @pl.when(j == 0)
    def _():
        zs_ref[0:SUBLANES, :] = jnp.zeros((SUBLANES, CONV_WIDTH), jnp.float32)
        k_ref[0:BLOCK, :] = jnp.zeros((BLOCK, KV_WIDTH), jnp.bfloat16)
        v_ref[0:BLOCK, :] = jnp.zeros((BLOCK, KV_WIDTH), jnp.bfloat16)

    lane = lax.broadcasted_iota(jnp.int32, (1, LANES), 1)
    lo_lanes = lane < HEAD_DIM

    def head_pair_norm(col, gain):
        sq = col * col
        lo = jnp.sum(jnp.where(lo_lanes, sq, 0.0), axis=-1, keepdims=True)
        hi = jnp.sum(jnp.where(lo_lanes, 0.0, sq), axis=-1, keepdims=True)
        ms = jnp.where(lo_lanes, lo, hi) * (1.0 / HEAD_DIM)
        return (col * lax.rsqrt(ms + EPS) * gain).astype(jnp.bfloat16)

    qn = qn_ref[...]
    h_blocks = []
    for r0 in range(0, tq, BLOCK):
        xb = x_ref[0, r0:r0 + BLOCK, :]
        hb = (xb * _rms_scale(xb) * nw_ref[...]).astype(jnp.bfloat16)
        h_blocks.append(hb)
        qkv = _dot(hb, win_ref[:, 3 * CONV_WIDTH:])
        for p in range(ATTN_WIDTH // LANES):
            cols = slice(p * LANES, (p + 1) * LANES)
            q_ref[r0:r0 + BLOCK, cols] = head_pair_norm(qkv[:, cols], qn[:, cols])
        k_ref[BLOCK + r0:2 * BLOCK + r0, :] = head_pair_norm(qkv[:, ATTN_WIDTH:ATTN_WIDTH + KV_WIDTH], kn_ref[...])
        v_ref[BLOCK + r0:2 * BLOCK + r0, :] = qkv[:, ATTN_WIDTH + KV_WIDTH:].astype(jnp.bfloat16)
    h = jnp.concatenate(h_blocks, axis=0)

    def conv_channels(i):
        c = slice(i * CONV_BLOCK, (i + 1) * CONV_BLOCK)
        gb, gc, vc = (_dot(h, win_ref[:, part * CONV_WIDTH + i * CONV_BLOCK:part * CONV_WIDTH + (i + 1) * CONV_BLOCK])
                      for part in range(3))
        z = gc * vc
        zs_ref[SUBLANES:SUBLANES + tq, c] = z
        y = z * convw_ref[CONV_K - 1:CONV_K, c]
        for tap in range(CONV_K - 1):
            back = CONV_K - 1 - tap
            y = y + zs_ref[SUBLANES - back:SUBLANES - back + tq, c] * convw_ref[tap:tap + 1, c]
        zs_ref[0:SUBLANES, c] = zs_ref[tq:tq + SUBLANES, c]
        act_ref[:, c] = (gb * y).astype(jnp.bfloat16)

    gate_chunks = []
    y_conv_chunks = []

    def gates_and_conv_out(i):
        gc = slice(i * GATE_CHUNK, (i + 1) * GATE_CHUNK)
        gate_chunks.append(jax.nn.sigmoid(_dot(h, wbg_ref[:, gc]) + bbg_ref[:, gc]))
        c = slice(i * CONV_BLOCK, (i + 1) * CONV_BLOCK)
        y_conv_chunks.append(_dot(act_ref[...], wco_ref[:, c]))

    n_conv_blocks = CONV_WIDTH // CONV_BLOCK
    dense_pieces = ([lambda i=i: conv_channels(i) for i in range(n_conv_blocks)]
                    + [lambda i=i: gates_and_conv_out(i) for i in range(n_conv_blocks)])

    kl = lax.broadcasted_iota(jnp.int32, (SM_ROWS, LANES), 1)
    qi0 = lax.broadcasted_iota(jnp.int32, (SM_ROWS, LANES), 0)
    slab_row = lax.broadcasted_iota(jnp.int32, (2 * 2 * BLOCK, LANES), 0)
    ones_blk = jnp.where((slab_row < 2 * BLOCK) == lo_lanes, 1.0, 0.0).astype(jnp.bfloat16)

    def pair_slab(ref, r0, g, extra=None):
        w = ref[r0:r0 + 2 * BLOCK, :]
        w_sw = jnp.concatenate([w[:, HEAD_DIM:], w[:, :HEAD_DIM]], axis=1)
        lo, hi = (w, w_sw) if g == 0 else (w_sw, w)
        zero = jnp.zeros_like(w)
        slab = jnp.concatenate([jnp.where(lo_lanes, lo, zero), jnp.where(lo_lanes, zero, hi)], axis=0)
        return slab if extra is None else jnp.concatenate([slab, extra], axis=1)

    def scores(n, g):
        r0, c0 = n * BLOCK, g * GROUP * HEAD_DIM
        qs = jnp.concatenate(
            [q_ref[r0:r0 + BLOCK, c0 + p * LANES:c0 + (p + 1) * LANES] for p in range(PAIRS)], axis=0)
        return _dot_nt(qs, pair_slab(k_ref, r0, g))

    def softmax(n, g, s, slot):
        prev_off = jnp.where(j > 0, 0, BLOCK) if n == 0 else 0
        for t in range(PAIRS * BLOCK // SM_ROWS):
            rs = t * SM_ROWS
            qi = qi0 + rs % BLOCK
            prev_ok = kl > qi + prev_off
            cur_ok = kl <= qi
            sink_terms = []
            for half in range(2):
                cb = half * 2 * BLOCK
                sp = jnp.where(prev_ok, s[rs:rs + SM_ROWS, cb:cb + BLOCK], NEG_INF)
                sc = jnp.where(cur_ok, s[rs:rs + SM_ROWS, cb + BLOCK:cb + 2 * BLOCK], NEG_INF)
                if bounded_scores:
                    m = sm_ref[2 * N_Q_HEADS]
                else:
                    sink = sm_ref[g * GROUP + 2 * (rs // BLOCK) + half]
                    m = jnp.maximum(jnp.max(jnp.maximum(sp, sc), axis=-1, keepdims=True), sink)
                    sink_terms.append(jnp.exp2(sink - m))
                e_ref[slot, rs:rs + SM_ROWS, cb:cb + BLOCK] = jnp.exp2(sp - m).astype(jnp.bfloat16)
                e_ref[slot, rs:rs + SM_ROWS, cb + BLOCK:cb + 2 * BLOCK] = jnp.exp2(sc - m).astype(jnp.bfloat16)
            if not bounded_scores:
                st_ref[slot, rs:rs + SM_ROWS, :] = jnp.where(lo_lanes, sink_terms[0], sink_terms[1])

    def weighted_values(n, g, slot):
        r0, c0 = n * BLOCK, g * GROUP * HEAD_DIM
        o = _dot(e_ref[slot], pair_slab(v_ref, r0, g, ones_blk))
        for p in range(PAIRS):
            rows = slice(p * BLOCK, (p + 1) * BLOCK)
            if bounded_scores:
                head = N_Q_HEADS + g * GROUP + 2 * p
                sink_term = jnp.where(lo_lanes, sm_ref[head], sm_ref[head + 1])
            else:
                sink_term = st_ref[slot, rows, :]
            op = o[rows, 0:LANES] / (o[rows, LANES:2 * LANES] + sink_term)
            attn_ref[r0:r0 + BLOCK, c0 + p * LANES:c0 + (p + 1) * LANES] = op.astype(jnp.bfloat16)

    steps = [(n, g) for n in range(tq // BLOCK) for g in range(N_KV_HEADS)]
    assert len(dense_pieces) == len(steps)
    s_next = scores(*steps[0])
    for i, (n, g) in enumerate(steps):
        s = s_next
        if i + 1 < len(steps):
            s_next = scores(*steps[i + 1])
        dense_pieces[i]()
        softmax(n, g, s, i % 2)
        weighted_values(n, g, i % 2)
    gates = jnp.concatenate(gate_chunks, axis=1)
    y_conv = jnp.concatenate(y_conv_chunks, axis=1)

    k_ref[0:BLOCK, :] = k_ref[tq:tq + BLOCK, :]
    v_ref[0:BLOCK, :] = v_ref[tq:tq + BLOCK, :]

    y_attn = _dot(attn_ref[...], wao_ref[...])
    merged = gates[:, 0:D_MODEL] * y_conv + gates[:, D_MODEL:] * y_attn
    o_ref[0] = x_ref[0] + _dot(merged.astype(jnp.bfloat16), wout_ref[...])
    _cast_blocks(cast_in, cast_out)


def _mixer(bounded_scores, sm, x, norm_w, w_in, conv_w, q_gain, k_gain, w_conv_out, w_attn_out,
           w_branch_gate, b_branch_gate, w_out, *cast):
    batch, seq, _ = x.shape
    tq = MIX_TILE
    assert seq % tq == 0 and tq % BLOCK == 0 and BLOCK % SM_ROWS == 0
    row_spec = pl.BlockSpec((1, tq, D_MODEL), lambda b, j: (b, j, 0))
    tiles = seq // tq
    cast_specs, cast_shapes = _cast_specs(cast, batch * tiles, lambda b, j: b * tiles + j)
    return pl.pallas_call(
        functools.partial(_mixer_kernel, bounded_scores=bounded_scores, n_cast=len(cast)),
        grid=(batch, tiles),
        in_specs=[
            pl.BlockSpec(memory_space=pltpu.SMEM),
            row_spec,
            _const_spec((1, D_MODEL)),
            _const_spec((D_MODEL, 3 * CONV_WIDTH + ATTN_WIDTH + 2 * KV_WIDTH)),
            _const_spec((CONV_K, CONV_WIDTH)),
            _const_spec((1, ATTN_WIDTH)),
            _const_spec((1, KV_WIDTH)),
            _const_spec((CONV_WIDTH, D_MODEL)),
            _const_spec((ATTN_WIDTH, D_MODEL)),
            _const_spec((D_MODEL, 2 * D_MODEL)),
            _const_spec((1, 2 * D_MODEL)),
            _const_spec((D_MODEL, D_MODEL)),
            *cast_specs,
        ],
        out_specs=[row_spec, *cast_specs],
        out_shape=[jax.ShapeDtypeStruct(x.shape, jnp.float32), *cast_shapes],
        scratch_shapes=[
            pltpu.VMEM((tq + SUBLANES, CONV_WIDTH), jnp.float32),
            pltpu.VMEM((tq, CONV_WIDTH), jnp.bfloat16),
            pltpu.VMEM((tq, ATTN_WIDTH), jnp.bfloat16),
            pltpu.VMEM((tq + BLOCK, KV_WIDTH), jnp.bfloat16),
            pltpu.VMEM((tq + BLOCK, KV_WIDTH), jnp.bfloat16),
            pltpu.VMEM((2, PAIRS * BLOCK, 4 * BLOCK), jnp.bfloat16),
            pltpu.VMEM((2, PAIRS * BLOCK, LANES), jnp.float32),
            pltpu.VMEM((tq, ATTN_WIDTH), jnp.bfloat16),
        ],
        compiler_params=pltpu.CompilerParams(
            dimension_semantics=("arbitrary", "arbitrary"), vmem_limit_bytes=VMEM_LIMIT_BYTES),
        name="mixer_bounded" if bounded_scores else "mixer",
    )(sm, x, norm_w, w_in, conv_w, q_gain, k_gain, w_conv_out, w_attn_out,
      w_branch_gate, b_branch_gate, w_out, *cast)


def kernel(x, ffn1_norm, ffn1_w_gate, ffn1_w_up, ffn1_w_down, mix_norm, w_in, conv_w, q_norm, k_norm, sinks, w_conv_out, w_attn_out, w_branch_gate, b_branch_gate, w_out, ffn2_norm, ffn2_w_gate, ffn2_w_up, ffn2_w_down):
    batch, seq, d = x.shape
    depth = ffn1_norm.shape[0]
    bf = lambda w: w.astype(jnp.bfloat16)
    q_gain_scale = HEAD_DIM ** -0.5 * LOG2E
    for l in range(depth):
        x, w_in_b, w_co_b, w_ao_b, w_bg_b, w_out_b = _ffn(
            x.reshape(batch * seq, d), ffn1_norm[l][None], bf(ffn1_w_gate[l]), bf(ffn1_w_up[l]), bf(ffn1_w_down[l]),
            cast=(w_in[l], w_conv_out[l], w_attn_out[l], w_branch_gate[l], w_out[l]))
        x = x.reshape(batch, seq, d)

        q_gain = q_norm[l] * q_gain_scale
        sinks2 = sinks[l] * LOG2E
        bound = HEAD_DIM * jnp.max(jnp.abs(q_gain)) * jnp.max(jnp.abs(k_norm[l])) * SCORE_BOUND_SLACK
        sm = jnp.concatenate([sinks2, jnp.exp2(sinks2 - bound), bound[None]])
        mixer_args = (sm, x, mix_norm[l][None], w_in_b, conv_w[l],
                      jnp.tile(q_gain, N_Q_HEADS)[None], jnp.tile(k_norm[l], N_KV_HEADS)[None],
                      w_co_b, w_ao_b, w_bg_b, b_branch_gate[l][None], w_out_b,
                      ffn2_w_gate[l], ffn2_w_up[l], ffn2_w_down[l])
        x, w_gate_b, w_up_b, w_down_b = lax.cond(
            bound <= SCORE_BOUND_MAX, functools.partial(_mixer, True), functools.partial(_mixer, False), *mixer_args)

        x, = _ffn(x.reshape(batch * seq, d), ffn2_norm[l][None], w_gate_b, w_up_b, w_down_b)
        x = x.reshape(batch, seq, d)
    return x
```

```python
import functools
import math

import jax
import jax.numpy as jnp
from jax import lax
from jax.experimental import pallas as pl
from jax.experimental.pallas import tpu as pltpu

D_MODEL = 1024
D_FF = 2816
CONV_WIDTH = D_MODEL
CONV_K = 3
N_Q_HEADS = 16
N_KV_HEADS = 2
HEAD_DIM = 64
GROUP = N_Q_HEADS // N_KV_HEADS
ATTN_WIDTH = N_Q_HEADS * HEAD_DIM
KV_WIDTH = N_KV_HEADS * HEAD_DIM
BLOCK = 128
FFN_RESIDUAL = 0.5
EPS = 1e-6
NEG_INF = -1e30
LOG2E = math.log2(math.e)

LANES = 128
SUBLANES = 8
BF16_ROWS = 2 * SUBLANES
PAIRS = GROUP // 2
VMEM_LIMIT_BYTES = 56 * 1024 * 1024

FFN_TILE = 1024
FFN_SUBTILE = 256
FFN_CHUNKS = (768, 768, 768, 512)
MIX_TILE = 512
SM_ROWS = 32
SCORE_BOUND_SLACK = 1.02
SCORE_BOUND_MAX = 30.0
CONV_BLOCK = 256
GATE_CHUNK = 2 * D_MODEL // (CONV_WIDTH // CONV_BLOCK)


def _rms_scale(xf):
    return lax.rsqrt(jnp.mean(xf * xf, axis=-1, keepdims=True) + EPS)


def _dot(a, b):
    return jnp.dot(a, b, preferred_element_type=jnp.float32)


def _dot_nt(a, b):
    return lax.dot_general(a, b, (((1,), (1,)), ((), ())), preferred_element_type=jnp.float32)


def _cast_plan(weight, n_steps):
    rows, cols = weight.shape
    rb = -(-rows // n_steps)
    rb = -(-rb // BF16_ROWS) * BF16_ROWS
    n_blocks, reps = rows // rb, n_steps * rb // rows
    assert rows % rb == 0 and reps >= 1
    return rb, cols, n_blocks, reps


def _cast_specs(weights, n_steps, step_of):
    specs, shapes = [], []
    for w in weights:
        rb, cols, n_blocks, reps = _cast_plan(w, n_steps)
        specs.append(pl.BlockSpec(
            (rb, cols),
            lambda *idx, reps=reps, last=n_blocks - 1: (jnp.minimum(step_of(*idx) // reps, last), 0)))
        shapes.append(jax.ShapeDtypeStruct(w.shape, jnp.bfloat16))
    return specs, shapes


def _cast_blocks(in_refs, out_refs):
    for src, dst in zip(in_refs, out_refs):
        dst[...] = src[...].astype(jnp.bfloat16)


def _ffn_kernel(x_ref, nw_ref, wg_ref, wu_ref, wd_ref, *rest):
    n_cast = len(rest) // 2
    o_ref = rest[n_cast]
    chunk_starts = [sum(FFN_CHUNKS[:c]) for c in range(len(FFN_CHUNKS))]
    stages = [(r0, c) for r0 in range(0, FFN_TILE, FFN_SUBTILE) for c in range(len(FFN_CHUNKS))]
    h_of = {}

    def gate_up(r0, c):
        if r0 not in h_of:
            x = x_ref[r0:r0 + FFN_SUBTILE, :]
            h_of[r0] = (x * _rms_scale(x) * nw_ref[...]).astype(jnp.bfloat16)
        cols = slice(chunk_starts[c], chunk_starts[c] + FFN_CHUNKS[c])
        return _dot(h_of[r0], wg_ref[:, cols]), _dot(h_of[r0], wu_ref[:, cols])

    gu_next = gate_up(*stages[0])
    acc = None
    for k, (r0, c) in enumerate(stages):
        g, u = gu_next
        if k + 1 < len(stages):
            gu_next = gate_up(*stages[k + 1])
        a = (jax.nn.silu(g) * u).astype(jnp.bfloat16)
        d = _dot(a, wd_ref[chunk_starts[c]:chunk_starts[c] + FFN_CHUNKS[c], :])
        acc = d if c == 0 else acc + d
        if c == len(FFN_CHUNKS) - 1:
            o_ref[r0:r0 + FFN_SUBTILE, :] = x_ref[r0:r0 + FFN_SUBTILE, :] + FFN_RESIDUAL * acc
    _cast_blocks(rest[:n_cast], rest[n_cast + 1:])


def _const_spec(shape):
    return pl.BlockSpec(shape, lambda *_: (0,) * len(shape), pipeline_mode=pl.Buffered(1))


def _ffn(x2d, norm_w, w_gate, w_up, w_down, cast=()):
    tokens = x2d.shape[0]
    assert tokens % FFN_TILE == 0 and sum(FFN_CHUNKS) == D_FF
    n_steps = tokens // FFN_TILE
    row_spec = pl.BlockSpec((FFN_TILE, D_MODEL), lambda i: (i, 0))
    cast_specs, cast_shapes = _cast_specs(cast, n_steps, lambda i: i)
    return pl.pallas_call(
        _ffn_kernel,
        grid=(n_steps,),
        in_specs=[
            row_spec,
            _const_spec((1, D_MODEL)),
            _const_spec((D_MODEL, D_FF)),
            _const_spec((D_MODEL, D_FF)),
            _const_spec((D_FF, D_MODEL)),
            *cast_specs,
        ],
        out_specs=[row_spec, *cast_specs],
        out_shape=[jax.ShapeDtypeStruct(x2d.shape, jnp.float32), *cast_shapes],
        compiler_params=pltpu.CompilerParams(
            dimension_semantics=("arbitrary",), vmem_limit_bytes=VMEM_LIMIT_BYTES),
        name="ffn",
    )(x2d, norm_w, w_gate, w_up, w_down, *cast)


def _mixer_kernel(sm_ref, x_ref, nw_ref, win_ref, convw_ref, qn_ref, kn_ref,
                  wco_ref, wao_ref, wbg_ref, bbg_ref, wout_ref, *rest, bounded_scores, n_cast):
    cast_in, o_ref, cast_out = rest[:n_cast], rest[n_cast], rest[n_cast + 1:2 * n_cast + 1]
    zs_ref, act_ref, q_ref, k_ref, v_ref, e_ref, st_ref, attn_ref = rest[2 * n_cast + 1:]
    tq = MIX_TILE
    j = pl.program_id(1)

    @pl.when(j == 0)
    def _():
        zs_ref[0:SUBLANES, :] = jnp.zeros((SUBLANES, CONV_WIDTH), jnp.float32)
        k_ref[0:BLOCK, :] = jnp.zeros((BLOCK, KV_WIDTH), jnp.bfloat16)
        v_ref[0:BLOCK, :] = jnp.zeros((BLOCK, KV_WIDTH), jnp.bfloat16)

    lane = lax.broadcasted_iota(jnp.int32, (1, LANES), 1)
    lo_lanes = lane < HEAD_DIM

    def head_pair_norm(col, gain):
        sq = col * col
        lo = jnp.sum(jnp.where(lo_lanes, sq, 0.0), axis=-1, keepdims=True)
        hi = jnp.sum(jnp.where(lo_lanes, 0.0, sq), axis=-1, keepdims=True)
        ms = jnp.where(lo_lanes, lo, hi) * (1.0 / HEAD_DIM)
        return (col * lax.rsqrt(ms + EPS) * gain).astype(jnp.bfloat16)

    qn = qn_ref[...]
    h_blocks = []
    for r0 in range(0, tq, BLOCK):
        xb = x_ref[0, r0:r0 + BLOCK, :]
        hb = (xb * _rms_scale(xb) * nw_ref[...]).astype(jnp.bfloat16)
        h_blocks.append(hb)
        qkv = _dot(hb, win_ref[:, 3 * CONV_WIDTH:])
        for p in range(ATTN_WIDTH // LANES):
            cols = slice(p * LANES, (p + 1) * LANES)
            q_ref[r0:r0 + BLOCK, cols] = head_pair_norm(qkv[:, cols], qn[:, cols])
        k_ref[BLOCK + r0:2 * BLOCK + r0, :] = head_pair_norm(qkv[:, ATTN_WIDTH:ATTN_WIDTH + KV_WIDTH], kn_ref[...])
        v_ref[BLOCK + r0:2 * BLOCK + r0, :] = qkv[:, ATTN_WIDTH + KV_WIDTH:].astype(jnp.bfloat16)
    h = jnp.concatenate(h_blocks, axis=0)

    def conv_channels(i):
        c = slice(i * CONV_BLOCK, (i + 1) * CONV_BLOCK)
        gb, gc, vc = (_dot(h, win_ref[:, part * CONV_WIDTH + i * CONV_BLOCK:part * CONV_WIDTH + (i + 1) * CONV_BLOCK])
                      for part in range(3))
        z = gc * vc
        zs_ref[SUBLANES:SUBLANES + tq, c] = z
        y = z * convw_ref[CONV_K - 1:CONV_K, c]
        for tap in range(CONV_K - 1):
            back = CONV_K - 1 - tap
            y = y + zs_ref[SUBLANES - back:SUBLANES - back + tq, c] * convw_ref[tap:tap + 1, c]
        zs_ref[0:SUBLANES, c] = zs_ref[tq:tq + SUBLANES, c]
        act_ref[:, c] = (gb * y).astype(jnp.bfloat16)

    gate_chunks = []
    y_conv_chunks = []

    def gates_and_conv_out(i):
        gc = slice(i * GATE_CHUNK, (i + 1) * GATE_CHUNK)
        gate_chunks.append(jax.nn.sigmoid(_dot(h, wbg_ref[:, gc]) + bbg_ref[:, gc]))
        c = slice(i * CONV_BLOCK, (i + 1) * CONV_BLOCK)
        y_conv_chunks.append(_dot(act_ref[...], wco_ref[:, c]))

    n_conv_blocks = CONV_WIDTH // CONV_BLOCK
    dense_pieces = ([lambda i=i: conv_channels(i) for i in range(n_conv_blocks)]
                    + [lambda i=i: gates_and_conv_out(i) for i in range(n_conv_blocks)])

    kl = lax.broadcasted_iota(jnp.int32, (SM_ROWS, LANES), 1)
    qi0 = lax.broadcasted_iota(jnp.int32, (SM_ROWS, LANES), 0)
    slab_row = lax.broadcasted_iota(jnp.int32, (2 * 2 * BLOCK, LANES), 0)
    ones_blk = jnp.where((slab_row < 2 * BLOCK) == lo_lanes, 1.0, 0.0).astype(jnp.bfloat16)

    def pair_slab(ref, r0, g, extra=None):
        w = ref[r0:r0 + 2 * BLOCK, :]
        w_sw = jnp.concatenate([w[:, HEAD_DIM:], w[:, :HEAD_DIM]], axis=1)
        lo, hi = (w, w_sw) if g == 0 else (w_sw, w)
        zero = jnp.zeros_like(w)
        slab = jnp.concatenate([jnp.where(lo_lanes, lo, zero), jnp.where(lo_lanes, zero, hi)], axis=0)
        return slab if extra is None else jnp.concatenate([slab, extra], axis=1)

    def scores(n, g):
        r0, c0 = n * BLOCK, g * GROUP * HEAD_DIM
        qs = jnp.concatenate(
            [q_ref[r0:r0 + BLOCK, c0 + p * LANES:c0 + (p + 1) * LANES] for p in range(PAIRS)], axis=0)
        return _dot_nt(qs, pair_slab(k_ref, r0, g))

    def softmax(n, g, s, slot):
        prev_off = jnp.where(j > 0, 0, BLOCK) if n == 0 else 0
        for t in range(PAIRS * BLOCK // SM_ROWS):
            rs = t * SM_ROWS
            qi = qi0 + rs % BLOCK
            prev_ok = kl > qi + prev_off
            cur_ok = kl <= qi
            sink_terms = []
            for half in range(2):
                cb = half * 2 * BLOCK
                sp = jnp.where(prev_ok, s[rs:rs + SM_ROWS, cb:cb + BLOCK], NEG_INF)
                sc = jnp.where(cur_ok, s[rs:rs + SM_ROWS, cb + BLOCK:cb + 2 * BLOCK], NEG_INF)
                if bounded_scores:
                    m = sm_ref[2 * N_Q_HEADS]
                else:
                    sink = sm_ref[g * GROUP + 2 * (rs // BLOCK) + half]
                    m = jnp.maximum(jnp.max(jnp.maximum(sp, sc), axis=-1, keepdims=True), sink)
                    sink_terms.append(jnp.exp2(sink - m))
                e_ref[slot, rs:rs + SM_ROWS, cb:cb + BLOCK] = jnp.exp2(sp - m).astype(jnp.bfloat16)
                e_ref[slot, rs:rs + SM_ROWS, cb + BLOCK:cb + 2 * BLOCK] = jnp.exp2(sc - m).astype(jnp.bfloat16)
            if not bounded_scores:
                st_ref[slot, rs:rs + SM_ROWS, :] = jnp.where(lo_lanes, sink_terms[0], sink_terms[1])

    def weighted_values(n, g, slot):
        r0, c0 = n * BLOCK, g * GROUP * HEAD_DIM
        o = _dot(e_ref[slot], pair_slab(v_ref, r0, g, ones_blk))
        for p in range(PAIRS):
            rows = slice(p * BLOCK, (p + 1) * BLOCK)
            if bounded_scores:
                head = N_Q_HEADS + g * GROUP + 2 * p
                sink_term = jnp.where(lo_lanes, sm_ref[head], sm_ref[head + 1])
            else:
                sink_term = st_ref[slot, rows, :]
            op = o[rows, 0:LANES] / (o[rows, LANES:2 * LANES] + sink_term)
            attn_ref[r0:r0 + BLOCK, c0 + p * LANES:c0 + (p + 1) * LANES] = op.astype(jnp.bfloat16)

    steps = [(n, g) for n in range(tq // BLOCK) for g in range(N_KV_HEADS)]
    assert len(dense_pieces) == len(steps)
    s_next = scores(*steps[0])
    for i, (n, g) in enumerate(steps):
        s = s_next
        if i + 1 < len(steps):
            s_next = scores(*steps[i + 1])
        dense_pieces[i]()
        softmax(n, g, s, i % 2)
        weighted_values(n, g, i % 2)
    gates = jnp.concatenate(gate_chunks, axis=1)
    y_conv = jnp.concatenate(y_conv_chunks, axis=1)

    k_ref[0:BLOCK, :] = k_ref[tq:tq + BLOCK, :]
    v_ref[0:BLOCK, :] = v_ref[tq:tq + BLOCK, :]

    y_attn = _dot(attn_ref[...], wao_ref[...])
    merged = gates[:, 0:D_MODEL] * y_conv + gates[:, D_MODEL:] * y_attn
    o_ref[0] = x_ref[0] + _dot(merged.astype(jnp.bfloat16), wout_ref[...])
    _cast_blocks(cast_in, cast_out)


def _mixer(bounded_scores, sm, x, norm_w, w_in, conv_w, q_gain, k_gain, w_conv_out, w_attn_out,
           w_branch_gate, b_branch_gate, w_out, *cast):
    batch, seq, _ = x.shape
    tq = MIX_TILE
    assert seq % tq == 0 and tq % BLOCK == 0 and BLOCK % SM_ROWS == 0
    row_spec = pl.BlockSpec((1, tq, D_MODEL), lambda b, j: (b, j, 0))
    tiles = seq // tq
    cast_specs, cast_shapes = _cast_specs(cast, batch * tiles, lambda b, j: b * tiles + j)
    return pl.pallas_call(
        functools.partial(_mixer_kernel, bounded_scores=bounded_scores, n_cast=len(cast)),
        grid=(batch, tiles),
        in_specs=[
            pl.BlockSpec(memory_space=pltpu.SMEM),
            row_spec,
            _const_spec((1, D_MODEL)),
            _const_spec((D_MODEL, 3 * CONV_WIDTH + ATTN_WIDTH + 2 * KV_WIDTH)),
            _const_spec((CONV_K, CONV_WIDTH)),
            _const_spec((1, ATTN_WIDTH)),
            _const_spec((1, KV_WIDTH)),
            _const_spec((CONV_WIDTH, D_MODEL)),
            _const_spec((ATTN_WIDTH, D_MODEL)),
            _const_spec((D_MODEL, 2 * D_MODEL)),
            _const_spec((1, 2 * D_MODEL)),
            _const_spec((D_MODEL, D_MODEL)),
            *cast_specs,
        ],
        out_specs=[row_spec, *cast_specs],
        out_shape=[jax.ShapeDtypeStruct(x.shape, jnp.float32), *cast_shapes],
        scratch_shapes=[
            pltpu.VMEM((tq + SUBLANES, CONV_WIDTH), jnp.float32),
            pltpu.VMEM((tq, CONV_WIDTH), jnp.bfloat16),
            pltpu.VMEM((tq, ATTN_WIDTH), jnp.bfloat16),
            pltpu.VMEM((tq + BLOCK, KV_WIDTH), jnp.bfloat16),
            pltpu.VMEM((tq + BLOCK, KV_WIDTH), jnp.bfloat16),
            pltpu.VMEM((2, PAIRS * BLOCK, 4 * BLOCK), jnp.bfloat16),
            pltpu.VMEM((2, PAIRS * BLOCK, LANES), jnp.float32),
            pltpu.VMEM((tq, ATTN_WIDTH), jnp.bfloat16),
        ],
        compiler_params=pltpu.CompilerParams(
            dimension_semantics=("arbitrary", "arbitrary"), vmem_limit_bytes=VMEM_LIMIT_BYTES),
        name="mixer_bounded" if bounded_scores else "mixer",
    )(sm, x, norm_w, w_in, conv_w, q_gain, k_gain, w_conv_out, w_attn_out,
      w_branch_gate, b_branch_gate, w_out, *cast)


def kernel(x, ffn1_norm, ffn1_w_gate, ffn1_w_up, ffn1_w_down, mix_norm, w_in, conv_w, q_norm, k_norm, sinks, w_conv_out, w_attn_out, w_branch_gate, b_branch_gate, w_out, ffn2_norm, ffn2_w_gate, ffn2_w_up, ffn2_w_down):
    batch, seq, d = x.shape
    depth = ffn1_norm.shape[0]
    bf = lambda w: w.astype(jnp.bfloat16)
    q_gain_scale = HEAD_DIM ** -0.5 * LOG2E
    for l in range(depth):
        x, w_in_b, w_co_b, w_ao_b, w_bg_b, w_out_b = _ffn(
            x.reshape(batch * seq, d), ffn1_norm[l][None], bf(ffn1_w_gate[l]), bf(ffn1_w_up[l]), bf(ffn1_w_down[l]),
            cast=(w_in[l], w_conv_out[l], w_attn_out[l], w_branch_gate[l], w_out[l]))
        x = x.reshape(batch, seq, d)

        q_gain = q_norm[l] * q_gain_scale
        sinks2 = sinks[l] * LOG2E
        bound = HEAD_DIM * jnp.max(jnp.abs(q_gain)) * jnp.max(jnp.abs(k_norm[l])) * SCORE_BOUND_SLACK
        sm = jnp.concatenate([sinks2, jnp.exp2(sinks2 - bound), bound[None]])
        mixer_args = (sm, x, mix_norm[l][None], w_in_b, conv_w[l],
                      jnp.tile(q_gain, N_Q_HEADS)[None], jnp.tile(k_norm[l], N_KV_HEADS)[None],
                      w_co_b, w_ao_b, w_bg_b, b_branch_gate[l][None], w_out_b,
                      ffn2_w_gate[l], ffn2_w_up[l], ffn2_w_down[l])
        x, w_gate_b, w_up_b, w_down_b = lax.cond(
            bound <= SCORE_BOUND_MAX, functools.partial(_mixer, True), functools.partial(_mixer, False), *mixer_args)

        x, = _ffn(x.reshape(batch * seq, d), ffn2_norm[l][None], w_gate_b, w_up_b, w_down_b)
        x = x.reshape(batch, seq, d)
    return x
```

```python
import functools
import math

import jax
import jax.numpy as jnp
from jax import lax
from jax.experimental import pallas as pl
from jax.experimental.pallas import tpu as pltpu

D_MODEL = 1024
D_FF = 2816
CONV_WIDTH = D_MODEL
CONV_K = 3
N_Q_HEADS = 16
N_KV_HEADS = 2
HEAD_DIM = 64
GROUP = N_Q_HEADS // N_KV_HEADS
ATTN_WIDTH = N_Q_HEADS * HEAD_DIM
KV_WIDTH = N_KV_HEADS * HEAD_DIM
BLOCK = 128
FFN_RESIDUAL = 0.5
EPS = 1e-6
NEG_INF = -1e30
LOG2E = math.log2(math.e)

LANES = 128
SUBLANES = 8
BF16_ROWS = 2 * SUBLANES
PAIRS = GROUP // 2
VMEM_LIMIT_BYTES = 56 * 1024 * 1024

FFN_TILE = 2048
FFN_SUBTILE = 256
FFN_CHUNKS = (768, 768, 768, 512)
MIX_TILE = 512
SM_ROWS = 32
SCORE_BOUND_SLACK = 1.02
SCORE_BOUND_MAX = 30.0
CONV_BLOCK = 256
GATE_CHUNK = 2 * D_MODEL // (CONV_WIDTH // CONV_BLOCK)


def _rms_scale(xf):
    return lax.rsqrt(jnp.mean(xf * xf, axis=-1, keepdims=True) + EPS)


def _dot(a, b):
    return jnp.dot(a, b, preferred_element_type=jnp.float32)


def _dot_nt(a, b):
    return lax.dot_general(a, b, (((1,), (1,)), ((), ())), preferred_element_type=jnp.float32)


def _cast_plan(weight, n_steps):
    rows, cols = weight.shape
    rb = -(-rows // n_steps)
    rb = -(-rb // BF16_ROWS) * BF16_ROWS
    n_blocks, reps = rows // rb, n_steps * rb // rows
    assert rows % rb == 0 and reps >= 1
    return rb, cols, n_blocks, reps


def _cast_specs(weights, n_steps, step_of):
    specs, shapes = [], []
    for w in weights:
        rb, cols, n_blocks, reps = _cast_plan(w, n_steps)
        specs.append(pl.BlockSpec(
            (rb, cols),
            lambda *idx, reps=reps, last=n_blocks - 1: (jnp.minimum(step_of(*idx) // reps, last), 0)))
        shapes.append(jax.ShapeDtypeStruct(w.shape, jnp.bfloat16))
    return specs, shapes


def _cast_blocks(in_refs, out_refs):
    for src, dst in zip(in_refs, out_refs):
        dst[...] = src[...].astype(jnp.bfloat16)


def _ffn_kernel(x_ref, nw_ref, wg_ref, wu_ref, wd_ref, *rest):
    n_cast = len(rest) // 2
    o_ref = rest[n_cast]
    chunk_starts = [sum(FFN_CHUNKS[:c]) for c in range(len(FFN_CHUNKS))]
    stages = [(r0, c) for r0 in range(0, FFN_TILE, FFN_SUBTILE) for c in range(len(FFN_CHUNKS))]
    h_of = {}

    def gate_up(r0, c):
        if r0 not in h_of:
            x = x_ref[r0:r0 + FFN_SUBTILE, :]
            h_of[r0] = (x * _rms_scale(x) * nw_ref[...]).astype(jnp.bfloat16)
        cols = slice(chunk_starts[c], chunk_starts[c] + FFN_CHUNKS[c])
        return _dot(h_of[r0], wg_ref[:, cols]), _dot(h_of[r0], wu_ref[:, cols])

    gu_next = gate_up(*stages[0])
    acc = None
    for k, (r0, c) in enumerate(stages):
        g, u = gu_next
        if k + 1 < len(stages):
            gu_next = gate_up(*stages[k + 1])
        a = (jax.nn.silu(g) * u).astype(jnp.bfloat16)
        d = _dot(a, wd_ref[chunk_starts[c]:chunk_starts[c] + FFN_CHUNKS[c], :])
        acc = d if c == 0 else acc + d
        if c == len(FFN_CHUNKS) - 1:
            o_ref[r0:r0 + FFN_SUBTILE, :] = x_ref[r0:r0 + FFN_SUBTILE, :] + FFN_RESIDUAL * acc
    _cast_blocks(rest[:n_cast], rest[n_cast + 1:])


def _const_spec(shape):
    return pl.BlockSpec(shape, lambda *_: (0,) * len(shape), pipeline_mode=pl.Buffered(1))


def _ffn(x2d, norm_w, w_gate, w_up, w_down, cast=()):
    tokens = x2d.shape[0]
    assert tokens % FFN_TILE == 0 and sum(FFN_CHUNKS) == D_FF
    n_steps = tokens // FFN_TILE
    row_spec = pl.BlockSpec((FFN_TILE, D_MODEL), lambda i: (i, 0))
    cast_specs, cast_shapes = _cast_specs(cast, n_steps, lambda i: i)
    return pl.pallas_call(
        _ffn_kernel,
        grid=(n_steps,),
        in_specs=[
            row_spec,
            _const_spec((1, D_MODEL)),
            _const_spec((D_MODEL, D_FF)),
            _const_spec((D_MODEL, D_FF)),
            _const_spec((D_FF, D_MODEL)),
            *cast_specs,
        ],
        out_specs=[row_spec, *cast_specs],
        out_shape=[jax.ShapeDtypeStruct(x2d.shape, jnp.float32), *cast_shapes],
        compiler_params=pltpu.CompilerParams(
            dimension_semantics=("arbitrary",), vmem_limit_bytes=VMEM_LIMIT_BYTES),
        name="ffn",
    )(x2d, norm_w, w_gate, w_up, w_down, *cast)


def _mixer_kernel(sm_ref, x_ref, nw_ref, win_ref, convw_ref, qn_ref, kn_ref,
                  wco_ref, wao_ref, wbg_ref, bbg_ref, wout_ref, *rest, bounded_scores, n_cast):
    cast_in, o_ref, cast_out = rest[:n_cast], rest[n_cast], rest[n_cast + 1:2 * n_cast + 1]
    zs_ref, act_ref, q_ref, k_ref, v_ref, e_ref, st_ref, attn_ref = rest[2 * n_cast + 1:]
    tq = MIX_TILE
    j = pl.program_id(1)

    @pl.when(j == 0)
    def _():
        zs_ref[0:SUBLANES, :] = jnp.zeros((SUBLANES, CONV_WIDTH), jnp.float32)
        k_ref[0:BLOCK, :] = jnp.zeros((BLOCK, KV_WIDTH), jnp.bfloat16)
        v_ref[0:BLOCK, :] = jnp.zeros((BLOCK, KV_WIDTH), jnp.bfloat16)

    lane = lax.broadcasted_iota(jnp.int32, (1, LANES), 1)
    lo_lanes = lane < HEAD_DIM

    def head_pair_norm(col, gain):
        sq = col * col
        lo = jnp.sum(jnp.where(lo_lanes, sq, 0.0), axis=-1, keepdims=True)
        hi = jnp.sum(jnp.where(lo_lanes, 0.0, sq), axis=-1, keepdims=True)
        ms = jnp.where(lo_lanes, lo, hi) * (1.0 / HEAD_DIM)
        return (col * lax.rsqrt(ms + EPS) * gain).astype(jnp.bfloat16)

    qn = qn_ref[...]
    h_blocks = []
    for r0 in range(0, tq, BLOCK):
        xb = x_ref[0, r0:r0 + BLOCK, :]
        hb = (xb * _rms_scale(xb) * nw_ref[...]).astype(jnp.bfloat16)
        h_blocks.append(hb)
        qkv = _dot(hb, win_ref[:, 3 * CONV_WIDTH:])
        for p in range(ATTN_WIDTH // LANES):
            cols = slice(p * LANES, (p + 1) * LANES)
            q_ref[r0:r0 + BLOCK, cols] = head_pair_norm(qkv[:, cols], qn[:, cols])
        k_ref[BLOCK + r0:2 * BLOCK + r0, :] = head_pair_norm(qkv[:, ATTN_WIDTH:ATTN_WIDTH + KV_WIDTH], kn_ref[...])
        v_ref[BLOCK + r0:2 * BLOCK + r0, :] = qkv[:, ATTN_WIDTH + KV_WIDTH:].astype(jnp.bfloat16)
    h = jnp.concatenate(h_blocks, axis=0)

    def conv_channels(i):
        c = slice(i * CONV_BLOCK, (i + 1) * CONV_BLOCK)
        gb, gc, vc = (_dot(h, win_ref[:, part * CONV_WIDTH + i * CONV_BLOCK:part * CONV_WIDTH + (i + 1) * CONV_BLOCK])
                      for part in range(3))
        z = gc * vc
        zs_ref[SUBLANES:SUBLANES + tq, c] = z
        y = z * convw_ref[CONV_K - 1:CONV_K, c]
        for tap in range(CONV_K - 1):
            back = CONV_K - 1 - tap
            y = y + zs_ref[SUBLANES - back:SUBLANES - back + tq, c] * convw_ref[tap:tap + 1, c]
        zs_ref[0:SUBLANES, c] = zs_ref[tq:tq + SUBLANES, c]
        act_ref[:, c] = (gb * y).astype(jnp.bfloat16)

    gate_chunks = []
    y_conv_chunks = []

    def gates_and_conv_out(i):
        gc = slice(i * GATE_CHUNK, (i + 1) * GATE_CHUNK)
        gate_chunks.append(jax.nn.sigmoid(_dot(h, wbg_ref[:, gc]) + bbg_ref[:, gc]))
        c = slice(i * CONV_BLOCK, (i + 1) * CONV_BLOCK)
        y_conv_chunks.append(_dot(act_ref[...], wco_ref[:, c]))

    n_conv_blocks = CONV_WIDTH // CONV_BLOCK
    dense_pieces = ([lambda i=i: conv_channels(i) for i in range(n_conv_blocks)]
                    + [lambda i=i: gates_and_conv_out(i) for i in range(n_conv_blocks)])

    kl = lax.broadcasted_iota(jnp.int32, (SM_ROWS, LANES), 1)
    qi0 = lax.broadcasted_iota(jnp.int32, (SM_ROWS, LANES), 0)
    slab_row = lax.broadcasted_iota(jnp.int32, (2 * 2 * BLOCK, LANES), 0)
    ones_blk = jnp.where((slab_row < 2 * BLOCK) == lo_lanes, 1.0, 0.0).astype(jnp.bfloat16)

    def pair_slab(ref, r0, g, extra=None):
        w = ref[r0:r0 + 2 * BLOCK, :]
        w_sw = jnp.concatenate([w[:, HEAD_DIM:], w[:, :HEAD_DIM]], axis=1)
        lo, hi = (w, w_sw) if g == 0 else (w_sw, w)
        zero = jnp.zeros_like(w)
        slab = jnp.concatenate([jnp.where(lo_lanes, lo, zero), jnp.where(lo_lanes, zero, hi)], axis=0)
        return slab if extra is None else jnp.concatenate([slab, extra], axis=1)

    def scores(n, g):
        r0, c0 = n * BLOCK, g * GROUP * HEAD_DIM
        qs = jnp.concatenate(
            [q_ref[r0:r0 + BLOCK, c0 + p * LANES:c0 + (p + 1) * LANES] for p in range(PAIRS)], axis=0)
        return _dot_nt(qs, pair_slab(k_ref, r0, g))

    def softmax(n, g, s, slot):
        prev_off = jnp.where(j > 0, 0, BLOCK) if n == 0 else 0
        for t in range(PAIRS * BLOCK // SM_ROWS):
            rs = t * SM_ROWS
            qi = qi0 + rs % BLOCK
            prev_ok = kl > qi + prev_off
            cur_ok = kl <= qi
            sink_terms = []
            for half in range(2):
                cb = half * 2 * BLOCK
                sp = jnp.where(prev_ok, s[rs:rs + SM_ROWS, cb:cb + BLOCK], NEG_INF)
                sc = jnp.where(cur_ok, s[rs:rs + SM_ROWS, cb + BLOCK:cb + 2 * BLOCK], NEG_INF)
                if bounded_scores:
                    m = sm_ref[2 * N_Q_HEADS]
                else:
                    sink = sm_ref[g * GROUP + 2 * (rs // BLOCK) + half]
                    m = jnp.maximum(jnp.max(jnp.maximum(sp, sc), axis=-1, keepdims=True), sink)
                    sink_terms.append(jnp.exp2(sink - m))
                e_ref[slot, rs:rs + SM_ROWS, cb:cb + BLOCK] = jnp.exp2(sp - m).astype(jnp.bfloat16)
                e_ref[slot, rs:rs + SM_ROWS, cb + BLOCK:cb + 2 * BLOCK] = jnp.exp2(sc - m).astype(jnp.bfloat16)
            if not bounded_scores:
                st_ref[slot, rs:rs + SM_ROWS, :] = jnp.where(lo_lanes, sink_terms[0], sink_terms[1])

    def weighted_values(n, g, slot):
        r0, c0 = n * BLOCK, g * GROUP * HEAD_DIM
        o = _dot(e_ref[slot], pair_slab(v_ref, r0, g, ones_blk))
        for p in range(PAIRS):
            rows = slice(p * BLOCK, (p + 1) * BLOCK)
            if bounded_scores:
                head = N_Q_HEADS + g * GROUP + 2 * p
                sink_term = jnp.where(lo_lanes, sm_ref[head], sm_ref[head + 1])
            else:
                sink_term = st_ref[slot, rows, :]
            op = o[rows, 0:LANES] / (o[rows, LANES:2 * LANES] + sink_term)
            attn_ref[r0:r0 + BLOCK, c0 + p * LANES:c0 + (p + 1) * LANES] = op.astype(jnp.bfloat16)

    steps = [(n, g) for n in range(tq // BLOCK) for g in range(N_KV_HEADS)]
    assert len(dense_pieces) == len(steps)
    s_next = scores(*steps[0])
    for i, (n, g) in enumerate(steps):
        s = s_next
        if i + 1 < len(steps):
            s_next = scores(*steps[i + 1])
        dense_pieces[i]()
        softmax(n, g, s, i % 2)
        weighted_values(n, g, i % 2)
    gates = jnp.concatenate(gate_chunks, axis=1)
    y_conv = jnp.concatenate(y_conv_chunks, axis=1)

    k_ref[0:BLOCK, :] = k_ref[tq:tq + BLOCK, :]
    v_ref[0:BLOCK, :] = v_ref[tq:tq + BLOCK, :]

    y_attn = _dot(attn_ref[...], wao_ref[...])
    merged = gates[:, 0:D_MODEL] * y_conv + gates[:, D_MODEL:] * y_attn
    o_ref[0] = x_ref[0] + _dot(merged.astype(jnp.bfloat16), wout_ref[...])
    _cast_blocks(cast_in, cast_out)


def _mixer(bounded_scores, sm, x, norm_w, w_in, conv_w, q_gain, k_gain, w_conv_out, w_attn_out,
           w_branch_gate, b_branch_gate, w_out, *cast):
    batch, seq, _ = x.shape
    tq = MIX_TILE
    assert seq % tq == 0 and tq % BLOCK == 0 and BLOCK % SM_ROWS == 0
    row_spec = pl.BlockSpec((1, tq, D_MODEL), lambda b, j: (b, j, 0))
    tiles = seq // tq
    cast_specs, cast_shapes = _cast_specs(cast, batch * tiles, lambda b, j: b * tiles + j)
    return pl.pallas_call(
        functools.partial(_mixer_kernel, bounded_scores=bounded_scores, n_cast=len(cast)),
        grid=(batch, tiles),
        in_specs=[
            pl.BlockSpec(memory_space=pltpu.SMEM),
            row_spec,
            _const_spec((1, D_MODEL)),
            _const_spec((D_MODEL, 3 * CONV_WIDTH + ATTN_WIDTH + 2 * KV_WIDTH)),
            _const_spec((CONV_K, CONV_WIDTH)),
            _const_spec((1, ATTN_WIDTH)),
            _const_spec((1, KV_WIDTH)),
            _const_spec((CONV_WIDTH, D_MODEL)),
            _const_spec((ATTN_WIDTH, D_MODEL)),
            _const_spec((D_MODEL, 2 * D_MODEL)),
            _const_spec((1, 2 * D_MODEL)),
            _const_spec((D_MODEL, D_MODEL)),
            *cast_specs,
        ],
        out_specs=[row_spec, *cast_specs],
        out_shape=[jax.ShapeDtypeStruct(x.shape, jnp.float32), *cast_shapes],
        scratch_shapes=[
            pltpu.VMEM((tq + SUBLANES, CONV_WIDTH), jnp.float32),
            pltpu.VMEM((tq, CONV_WIDTH), jnp.bfloat16),
            pltpu.VMEM((tq, ATTN_WIDTH), jnp.bfloat16),
            pltpu.VMEM((tq + BLOCK, KV_WIDTH), jnp.bfloat16),
            pltpu.VMEM((tq + BLOCK, KV_WIDTH), jnp.bfloat16),
            pltpu.VMEM((2, PAIRS * BLOCK, 4 * BLOCK), jnp.bfloat16),
            pltpu.VMEM((2, PAIRS * BLOCK, LANES), jnp.float32),
            pltpu.VMEM((tq, ATTN_WIDTH), jnp.bfloat16),
        ],
        compiler_params=pltpu.CompilerParams(
            dimension_semantics=("arbitrary", "arbitrary"), vmem_limit_bytes=VMEM_LIMIT_BYTES),
        name="mixer_bounded" if bounded_scores else "mixer",
    )(sm, x, norm_w, w_in, conv_w, q_gain, k_gain, w_conv_out, w_attn_out,
      w_branch_gate, b_branch_gate, w_out, *cast)


def kernel(x, ffn1_norm, ffn1_w_gate, ffn1_w_up, ffn1_w_down, mix_norm, w_in, conv_w, q_norm, k_norm, sinks, w_conv_out, w_attn_out, w_branch_gate, b_branch_gate, w_out, ffn2_norm, ffn2_w_gate, ffn2_w_up, ffn2_w_down):
    batch, seq, d = x.shape
    depth = ffn1_norm.shape[0]
    bf = lambda w: w.astype(jnp.bfloat16)
    q_gain_scale = HEAD_DIM ** -0.5 * LOG2E
    for l in range(depth):
        x, w_in_b, w_co_b, w_ao_b, w_bg_b, w_out_b = _ffn(
            x.reshape(batch * seq, d), ffn1_norm[l][None], bf(ffn1_w_gate[l]), bf(ffn1_w_up[l]), bf(ffn1_w_down[l]),
            cast=(w_in[l], w_conv_out[l], w_attn_out[l], w_branch_gate[l], w_out[l]))
        x = x.reshape(batch, seq, d)

        q_gain = q_norm[l] * q_gain_scale
        sinks2 = sinks[l] * LOG2E
        bound = HEAD_DIM * jnp.max(jnp.abs(q_gain)) * jnp.max(jnp.abs(k_norm[l])) * SCORE_BOUND_SLACK
        sm = jnp.concatenate([sinks2, jnp.exp2(sinks2 - bound), bound[None]])
        mixer_args = (sm, x, mix_norm[l][None], w_in_b, conv_w[l],
                      jnp.tile(q_gain, N_Q_HEADS)[None], jnp.tile(k_norm[l], N_KV_HEADS)[None],
                      w_co_b, w_ao_b, w_bg_b, b_branch_gate[l][None], w_out_b,
                      ffn2_w_gate[l], ffn2_w_up[l], ffn2_w_down[l])
        x, w_gate_b, w_up_b, w_down_b = lax.cond(
            bound <= SCORE_BOUND_MAX, functools.partial(_mixer, True), functools.partial(_mixer, False), *mixer_args)

        x, = _ffn(x.reshape(batch * seq, d), ffn2_norm[l][None], w_gate_b, w_up_b, w_down_b)
        x = x.reshape(batch, seq, d)
    return x
```
